```python
import jax, jax.numpy as jnp
from jax import lax
import numpy as np

D_MODEL = 4096
BATCH = 4
SEQ = 2048
DEPTH = 4
DEC_BATCH = 8
DEC_SEQ = 1
PAST_LEN = 8192
PAGE_SIZE = 128

N_A_LAYERS = DEPTH // 2
N_B_LAYERS = DEPTH - N_A_LAYERS
D_RNN = D_MODEL
N_LRU_BLOCKS = 16
LRU_BLOCK = D_RNN // N_LRU_BLOCKS
CONV_W = 4
LRU_C = 8.0
N_HEADS = 32
HEAD_DIM = D_MODEL // N_HEADS
D_ATTN = N_HEADS * HEAD_DIM
D_FF = 4 * D_MODEL
Q_BLOCK = 128
RMS_EPS = 1e-6
FORGET_BIAS_MEAN = 4.0

kernel_name = "yoco_hawk_fox_decode_step"


def _rmsnorm(x, g):
    xf = x.astype(jnp.float32)
    y = xf * lax.rsqrt(jnp.mean(xf * xf, axis=-1, keepdims=True) + RMS_EPS)
    return (y * g.astype(jnp.float32)).astype(x.dtype)


def _sqrelu_mlp(x, w_up, w_down):
    h = jax.nn.relu(x @ w_up)
    return (h * h) @ w_down


def _rglru_block(x, h0, conv_buf, w_in, conv_w, conv_b, gate_w, gate_b, lam, w_out):
    B, T, _ = x.shape
    xy = x @ w_in
    xb, yb = xy[..., :D_RNN], xy[..., D_RNN:]
    xp = jnp.concatenate([conv_buf.astype(xb.dtype), xb], axis=1)
    conv = conv_b + sum(xp[:, k:k + T] * conv_w[k] for k in range(CONV_W))
    new_buf = xp[:, T:]
    cb = conv.reshape(B, T, N_LRU_BLOCKS, LRU_BLOCK)
    g = jnp.einsum('btnc,gncd->gbtnd', cb, gate_w).reshape(2, B, T, D_RNN)
    g = jax.nn.sigmoid(g.astype(jnp.float32) + gate_b.astype(jnp.float32)[:, None, None, :])
    i_gate, r_gate = g[0], g[1]
    log_a = LRU_C * r_gate * jax.nn.log_sigmoid(lam.astype(jnp.float32))
    a = jnp.exp(log_a)
    mult = jnp.sqrt(-jnp.expm1(2.0 * log_a))
    b = mult * i_gate * conv.astype(jnp.float32)
    b = b.at[:, 0].add(a[:, 0] * h0.astype(jnp.float32))

    def comb(l, r):
        return (l[0] * r[0], r[0] * l[1] + r[1])

    _, h = lax.associative_scan(comb, (a, b), axis=1)
    out = (h.astype(x.dtype) * jax.nn.gelu(yb)) @ w_out
    return out, h[:, -1].astype(x.dtype), new_buf


def _shared_kv(x, norm_kv, w_kvf, b_f):
    B, T, _ = x.shape
    kvf = _rmsnorm(x, norm_kv) @ w_kvf
    k = kvf[..., :D_ATTN].reshape(B, T, N_HEADS, HEAD_DIM)
    v = kvf[..., D_ATTN:2 * D_ATTN].reshape(B, T, N_HEADS, HEAD_DIM)
    logf = jax.nn.log_sigmoid(kvf[..., 2 * D_ATTN:].astype(jnp.float32) + b_f.astype(jnp.float32))
    return k, v, logf


def _fox_attend(q, cum_q, q_pos0, segs):
    B, Tq, H, hd = q.shape
    blk = Q_BLOCK if Tq % Q_BLOCK == 0 else Tq
    nb = Tq // blk
    scale = hd ** -0.5
    qb = jnp.moveaxis(q.reshape(B, nb, blk, H, hd), 1, 0)
    cb = jnp.moveaxis(cum_q.reshape(B, nb, blk, H), 1, 0)
    sizes = [s[0].shape[1] for s in segs]

    def one(args):
        qi, ci, bi = args
        q_pos = q_pos0 + bi * blk + jnp.arange(blk)
        scores = []
        for k, v, ck, kp0 in segs:
            s = jnp.einsum('bqhd,bshd->bhqs', qi, k, preferred_element_type=jnp.float32) * scale
            s = s + (jnp.swapaxes(ci, 1, 2)[..., :, None] - jnp.swapaxes(ck, 1, 2)[..., None, :])
            k_pos = kp0 + jnp.arange(k.shape[1])
            s = jnp.where(k_pos[None, :] <= q_pos[:, None], s, -jnp.inf)
            scores.append(s)
        p = jax.nn.softmax(jnp.concatenate(scores, axis=-1), axis=-1)
        out = None
        off = 0
        for (k, v, ck, kp0), n in zip(segs, sizes):
            term = jnp.einsum('bhqs,bshd->bqhd', p[..., off:off + n].astype(v.dtype), v)
            out = term if out is None else out + term
            off += n
        return out

    o = lax.map(one, (qb, cb, jnp.arange(nb)))
    return jnp.moveaxis(o, 0, 1).reshape(B, Tq, H, hd)


def _run_group(x, h0, conv0, past, q_pos0, p):
    B, T, _ = x.shape
    hs, convs = [], []
    segs, cum_new, k_new, v_new, logf_new = None, None, None, None, None
    for l in range(DEPTH):
        xn = _rmsnorm(x, p['norm_mix_pre'][l])
        if l < N_A_LAYERS:
            m, h_l, c_l = _rglru_block(xn, h0[l], conv0[l], p['w_in'][l], p['conv_w'][l], p['conv_b'][l],
                                       p['gate_w'][l], p['gate_b'][l], p['rg_lambda'][l], p['w_out_a'][l])
            hs.append(h_l)
            convs.append(c_l)
        else:
            if l == N_A_LAYERS:
                k_new, v_new, logf_new = _shared_kv(x, p['norm_kv'], p['w_kvf'], p['b_f'])
                if past is None:
                    cum_new = jnp.cumsum(logf_new, axis=1)
                    segs = [(k_new, v_new, cum_new, 0)]
                else:
                    k_past, v_past, logf_past = past
                    cum_past = jnp.cumsum(logf_past.astype(jnp.float32), axis=1)
                    cum_new = cum_past[:, -1:] + jnp.cumsum(logf_new, axis=1)
                    segs = [(k_past, v_past, cum_past, 0), (k_new, v_new, cum_new, q_pos0)]
            j = l - N_A_LAYERS
            q = (xn @ p['w_q'][j]).reshape(B, T, N_HEADS, HEAD_DIM)
            m = _fox_attend(q, cum_new, q_pos0, segs).reshape(B, T, D_ATTN) @ p['w_o'][j]
        x = x + _rmsnorm(m, p['norm_mix_post'][l])
        f = _sqrelu_mlp(_rmsnorm(x, p['norm_mlp_pre'][l]), p['w_up'][l], p['w_down'][l])
        x = x + _rmsnorm(f, p['norm_mlp_post'][l])
    return x, jnp.stack(hs), jnp.stack(convs), k_new, v_new, logf_new


def setup_inputs(seed: int = 0) -> dict:
    key = jax.random.key(seed)
    ks = jax.random.split(key, 32)
    f32 = jnp.float32

    def nrm(k, shape, s):
        return jax.random.normal(k, shape, f32) * s

    n_pages = PAST_LEN // PAGE_SIZE
    n_used = DEC_BATCH * n_pages
    n_pool = n_used + max(1, n_used // 4)
    a0 = jax.random.uniform(ks[12], (N_A_LAYERS, D_RNN), f32, 0.9, 0.999)
    return {
        'x_prompt': nrm(ks[0], (BATCH, SEQ, D_MODEL), 1.0),
        'x_sample': nrm(ks[1], (DEC_BATCH, DEC_SEQ, D_MODEL), 1.0),
        'state_rglru_h': nrm(ks[2], (N_A_LAYERS, DEC_BATCH, D_RNN), 0.5),
        'state_conv': nrm(ks[3], (N_A_LAYERS, DEC_BATCH, CONV_W - 1, D_RNN), 1.0),
        'cache_k': nrm(ks[4], (n_pool, PAGE_SIZE, N_HEADS, HEAD_DIM), 1.0),
        'cache_v': nrm(ks[5], (n_pool, PAGE_SIZE, N_HEADS, HEAD_DIM), 1.0),
        'cache_logf': jax.nn.log_sigmoid(FORGET_BIAS_MEAN + nrm(ks[6], (n_pool, PAGE_SIZE, N_HEADS), 1.0)),
        'page_table': jax.random.permutation(ks[7], n_pool)[:n_used].reshape(DEC_BATCH, n_pages).astype(jnp.int32),
        'w_in': nrm(ks[8], (N_A_LAYERS, D_MODEL, 2 * D_RNN), D_MODEL ** -0.5),
        'conv_w': nrm(ks[9], (N_A_LAYERS, CONV_W, D_RNN), CONV_W ** -0.5),
        'conv_b': nrm(ks[10], (N_A_LAYERS, D_RNN), 0.01),
        'gate_w': nrm(ks[11], (N_A_LAYERS, 2, N_LRU_BLOCKS, LRU_BLOCK, LRU_BLOCK), LRU_BLOCK ** -0.5),
        'gate_b': nrm(ks[13], (N_A_LAYERS, 2, D_RNN), 0.01),
        'rg_lambda': jnp.log(a0) - jnp.log1p(-a0),
        'w_out_a': nrm(ks[14], (N_A_LAYERS, D_RNN, D_MODEL), D_RNN ** -0.5),
        'norm_kv': 1.0 + nrm(ks[15], (D_MODEL,), 0.02),
        'w_kvf': nrm(ks[16], (D_MODEL, 2 * D_ATTN + N_HEADS), D_MODEL ** -0.5),
        'b_f': FORGET_BIAS_MEAN + nrm(ks[17], (N_HEADS,), 0.5),
        'w_q': nrm(ks[18], (N_B_LAYERS, D_MODEL, D_ATTN), D_MODEL ** -0.5),
        'w_o': nrm(ks[19], (N_B_LAYERS, D_ATTN, D_MODEL), D_ATTN ** -0.5),
        'norm_mix_pre': 1.0 + nrm(ks[20], (DEPTH, D_MODEL), 0.02),
        'norm_mix_post': 1.0 + nrm(ks[21], (DEPTH, D_MODEL), 0.02),
        'norm_mlp_pre': 1.0 + nrm(ks[22], (DEPTH, D_MODEL), 0.02),
        'norm_mlp_post': 1.0 + nrm(ks[23], (DEPTH, D_MODEL), 0.02),
        'w_up': nrm(ks[24], (DEPTH, D_MODEL, D_FF), D_MODEL ** -0.5),
        'w_down': nrm(ks[25], (DEPTH, D_FF, D_MODEL), D_FF ** -0.5),
    }


def reference(x_prompt, x_sample, state_rglru_h, state_conv, cache_k, cache_v, cache_logf, page_table,
              w_in, conv_w, conv_b, gate_w, gate_b, rg_lambda, w_out_a, norm_kv, w_kvf, b_f, w_q, w_o,
              norm_mix_pre, norm_mix_post, norm_mlp_pre, norm_mlp_post, w_up, w_down):
    p = dict(w_in=w_in, conv_w=conv_w, conv_b=conv_b, gate_w=gate_w, gate_b=gate_b, rg_lambda=rg_lambda,
             w_out_a=w_out_a, norm_kv=norm_kv, w_kvf=w_kvf, b_f=b_f, w_q=w_q, w_o=w_o,
             norm_mix_pre=norm_mix_pre, norm_mix_post=norm_mix_post, norm_mlp_pre=norm_mlp_pre,
             norm_mlp_post=norm_mlp_post, w_up=w_up, w_down=w_down)
    B = x_prompt.shape[0]
    h0 = jnp.zeros((N_A_LAYERS, B, D_RNN), x_prompt.dtype)
    c0 = jnp.zeros((N_A_LAYERS, B, CONV_W - 1, D_RNN), x_prompt.dtype)
    y_prompt, h_prompt, conv_prompt, k_prompt, v_prompt, logf_prompt = _run_group(x_prompt, h0, c0, None, 0, p)
    DB, n_pages = page_table.shape
    past_len = n_pages * PAGE_SIZE
    k_past = cache_k[page_table].reshape(DB, past_len, N_HEADS, HEAD_DIM)
    v_past = cache_v[page_table].reshape(DB, past_len, N_HEADS, HEAD_DIM)
    logf_past = cache_logf[page_table].reshape(DB, past_len, N_HEADS)
    y_sample, h_sample, conv_sample, k_sample, v_sample, logf_sample = _run_group(
        x_sample, state_rglru_h, state_conv, (k_past, v_past, logf_past), past_len, p)
    return (y_prompt, y_sample, h_prompt, conv_prompt, k_prompt, v_prompt, logf_prompt,
            h_sample, conv_sample, k_sample, v_sample, logf_sample)
```

```python
import functools
import math

import jax
import jax.numpy as jnp
from jax import lax
from jax.experimental import pallas as pl
from jax.experimental.pallas import tpu as pltpu

F32 = jnp.float32
BF16 = jnp.bfloat16

RMS_EPS = 1e-6
LRU_C = 8.0

V7X_VMEM_BYTES = 64 * 1024 * 1024
V7X_LANES = 128
V7X_SUBLANES = 8
VMEM_LIMIT = 56 * 1024 * 1024


def _params(*sem):
    return pltpu.CompilerParams(dimension_semantics=sem, vmem_limit_bytes=VMEM_LIMIT)


def _rms_scale(x):
    return x * lax.rsqrt(jnp.mean(x * x, axis=-1, keepdims=True) + RMS_EPS)


def _log_sigmoid(x):
    return jnp.minimum(x, 0.0) - jnp.log1p(jnp.exp(-jnp.abs(x)))


def _gelu_tanh(x):
    c = math.sqrt(2.0 / math.pi)
    return 0.5 * x * (1.0 + jnp.tanh(c * (x + 0.044715 * (x * x * x))))


def _norm_cast_kernel(x_ref, g_ref, o_ref):
    o_ref[...] = (_rms_scale(x_ref[...]) * g_ref[...]).astype(o_ref.dtype)


def _norm_cast(x, g):
    M, D = x.shape
    tr = min(256, M)
    return pl.pallas_call(
        _norm_cast_kernel,
        grid=(M // tr,),
        in_specs=[pl.BlockSpec((tr, D), lambda i: (i, 0)), pl.BlockSpec((1, D), lambda i: (0, 0))],
        out_specs=pl.BlockSpec((tr, D), lambda i: (i, 0)),
        out_shape=jax.ShapeDtypeStruct((M, D), BF16),
        compiler_params=_params("parallel"),
        name="norm_cast",
    )(x, g.reshape(1, D))


def _resid_norm_kernel(x_ref, m_ref, gp_ref, gn_ref, xo_ref, *n_refs):
    xn = x_ref[...] + _rms_scale(m_ref[...]) * gp_ref[...]
    xo_ref[...] = xn
    if n_refs:
        y = _rms_scale(xn)
        for j, r in enumerate(n_refs):
            r[...] = (y * gn_ref[j:j + 1, :]).astype(r.dtype)


def _resid_norm(x, m, g_post, g_next):
    M, D = x.shape
    n = len(g_next)
    tr = min(256, M)
    gn = jnp.stack(g_next) if n else jnp.zeros((1, D), F32)
    row = pl.BlockSpec((tr, D), lambda i: (i, 0))
    outs = pl.pallas_call(
        _resid_norm_kernel,
        grid=(M // tr,),
        in_specs=[row, row, pl.BlockSpec((1, D), lambda i: (0, 0)),
                  pl.BlockSpec((gn.shape[0], D), lambda i: (0, 0))],
        out_specs=[row] * (1 + n),
        out_shape=[jax.ShapeDtypeStruct((M, D), F32)] + [jax.ShapeDtypeStruct((M, D), BF16)] * n,
        compiler_params=_params("parallel"),
        name="resid_norm",
    )(x, m, g_post.reshape(1, D), gn)
    return outs[0], list(outs[1:])


def _mm_store(acc, o_refs, epilogue):
    if epilogue == "f32":
        o_refs[0][...] = acc
    elif epilogue == "bf16":
        o_refs[0][...] = acc.astype(BF16)
    elif epilogue == "f32_bf16":
        o_refs[0][...] = acc
        o_refs[1][...] = acc.astype(BF16)
    elif epilogue == "relu2_bf16":
        h = jnp.maximum(acc, 0.0)
        o_refs[0][...] = (h * h).astype(BF16)
    else:
        raise ValueError(epilogue)


def _mm_fullk_kernel(a_ref, w_ref, *rest, epilogue):
    *o_refs, wb_ref = rest

    @pl.when(pl.program_id(1) == 0)
    def _():
        wb_ref[...] = w_ref[...].astype(BF16)

    acc = jnp.dot(a_ref[...], wb_ref[...], preferred_element_type=F32)
    _mm_store(acc, o_refs, epilogue)


_EPILOGUE_DTYPES = {"f32": (F32,), "bf16": (BF16,), "f32_bf16": (F32, BF16), "relu2_bf16": (BF16,)}


def _mm_fullk(a, w, *, n_cols=None, col_block0=0, epilogue="f32"):
    M, K = a.shape
    N = w.shape[1] if n_cols is None else n_cols
    tm = min(1024, M)
    tn = min(512, N)
    assert M % tm == 0 and N % tn == 0
    dts = _EPILOGUE_DTYPES[epilogue]
    out_spec = pl.BlockSpec((tm, tn), lambda n, m: (m, n))
    outs = pl.pallas_call(
        functools.partial(_mm_fullk_kernel, epilogue=epilogue),
        grid=(N // tn, M // tm),
        in_specs=[pl.BlockSpec((tm, K), lambda n, m: (m, 0)),
                  pl.BlockSpec((K, tn), lambda n, m: (0, n + col_block0))],
        out_specs=[out_spec] * len(dts),
        out_shape=[jax.ShapeDtypeStruct((M, N), d) for d in dts],
        scratch_shapes=[pltpu.VMEM((K, tn), BF16)],
        compiler_params=_params("arbitrary", "arbitrary"),
        name="mm_fullk_" + epilogue,
    )(a, w)
    return outs if len(dts) > 1 else outs[0]


def _mm_ktiled_kernel(a_ref, w_ref, o_ref):
    part = jnp.dot(a_ref[...], w_ref[...].astype(BF16), preferred_element_type=F32)

    @pl.when(pl.program_id(2) == 0)
    def _():
        o_ref[...] = part

    @pl.when(pl.program_id(2) != 0)
    def _():
        o_ref[...] += part


def _mm_ktiled(a, w):
    M, K = a.shape
    N = w.shape[1]
    tm = min(1024, M)
    tn = min(1024, N)
    tk = min(2048, K)
    assert M % tm == 0 and N % tn == 0 and K % tk == 0
    return pl.pallas_call(
        _mm_ktiled_kernel,
        grid=(M // tm, N // tn, K // tk),
        in_specs=[pl.BlockSpec((tm, tk), lambda m, n, k: (m, k)),
                  pl.BlockSpec((tk, tn), lambda m, n, k: (k, n))],
        out_specs=pl.BlockSpec((tm, tn), lambda m, n, k: (m, n)),
        out_shape=jax.ShapeDtypeStruct((M, N), F32),
        compiler_params=_params("parallel", "parallel", "arbitrary"),
        name="mm_ktiled",
    )(a, w)


def _rglru_coeffs(conv, gw_ref, gb_ref, lam_ref):
    cb = conv.astype(BF16)
    gi = jnp.dot(cb, gw_ref[0, 0].astype(BF16), preferred_element_type=F32) + gb_ref[0:1, :]
    gr = jnp.dot(cb, gw_ref[1, 0].astype(BF16), preferred_element_type=F32) + gb_ref[1:2, :]
    i_gate = jax.nn.sigmoid(gi)
    r_gate = jax.nn.sigmoid(gr)
    log_a = LRU_C * r_gate * _log_sigmoid(lam_ref[...])
    a = jnp.exp(log_a)
    th = jnp.tanh(log_a)
    mult = jnp.sqrt(-2.0 * th / (1.0 - th))
    return a, mult * i_gate * conv


def _rglru_seq_kernel(x_ref, y_ref, h0_ref, c0_ref, cw_ref, cb_ref, gw_ref, gb_ref, lam_ref,
                      hg_ref, hl_ref, xs_ref, a_ref, b_ref, hc_ref, *, tT):
    t = pl.program_id(2)
    S = V7X_SUBLANES

    @pl.when(t == 0)
    def _():
        xs_ref[0:S, :] = c0_ref[0]
        hc_ref[...] = h0_ref[0]

    x = x_ref[0]
    xs_ref[S:S + tT, :] = x
    cw = cw_ref[...]
    nw = cw.shape[0]
    conv = cb_ref[...] + cw[nw - 1:nw, :] * x
    for j in range(1, nw):
        conv = conv + cw[nw - 1 - j:nw - j, :] * xs_ref[S - j:S - j + tT, :]
    xs_ref[0:S, :] = xs_ref[tT:tT + S, :]

    a, b = _rglru_coeffs(conv, gw_ref, gb_ref, lam_ref)
    a_ref[...] = a
    b_ref[...] = b

    row = lax.broadcasted_iota(jnp.int32, (S, a.shape[1]), 0)

    def group(i, h):
        r = pl.multiple_of(i * S, S)
        a8 = a_ref[pl.ds(r, S), :]
        b8 = b_ref[pl.ds(r, S), :]
        s = 1
        while s < S:
            keep = row >= s
            b8 = jnp.where(keep, a8 * pltpu.roll(b8, s, 0) + b8, b8)
            a8 = jnp.where(keep, a8 * pltpu.roll(a8, s, 0), a8)
            s *= 2
        h8 = a8 * h + b8
        b_ref[pl.ds(r, S), :] = h8
        return h8[S - 1:S, :]

    h_last = lax.fori_loop(0, tT // S, group, hc_ref[...])
    hc_ref[...] = h_last
    hl_ref[0] = h_last
    hg_ref[0] = (b_ref[...] * _gelu_tanh(y_ref[0])).astype(hg_ref.dtype)


def _rglru_seq(xy, h0, conv0, conv_w, conv_b, gate_w, gate_b, lam):
    B, T, C2 = xy.shape
    C = C2 // 2
    nblk, cblk = gate_w.shape[1], gate_w.shape[2]
    nw = conv_w.shape[0]
    S = V7X_SUBLANES
    tT = min(512, T)
    assert T % tT == 0 and tT % S == 0 and nw - 1 <= S and cblk * nblk == C
    c0 = jnp.concatenate([jnp.zeros((B, S - (nw - 1), C), F32), conv0], axis=1)
    hg, hl = pl.pallas_call(
        functools.partial(_rglru_seq_kernel, tT=tT),
        grid=(B, nblk, T // tT),
        in_specs=[
            pl.BlockSpec((1, tT, cblk), lambda b, n, t: (b, t, n)),
            pl.BlockSpec((1, tT, cblk), lambda b, n, t: (b, t, n + nblk)),
            pl.BlockSpec((1, 1, cblk), lambda b, n, t: (b, 0, n)),
            pl.BlockSpec((1, S, cblk), lambda b, n, t: (b, 0, n)),
            pl.BlockSpec((nw, cblk), lambda b, n, t: (0, n)),
            pl.BlockSpec((1, cblk), lambda b, n, t: (0, n)),
            pl.BlockSpec((2, 1, cblk, cblk), lambda b, n, t: (0, n, 0, 0)),
            pl.BlockSpec((2, cblk), lambda b, n, t: (0, n)),
            pl.BlockSpec((1, cblk), lambda b, n, t: (0, n)),
        ],
        out_specs=[pl.BlockSpec((1, tT, cblk), lambda b, n, t: (b, t, n)),
                   pl.BlockSpec((1, 1, cblk), lambda b, n, t: (b, 0, n))],
        out_shape=[jax.ShapeDtypeStruct((B, T, C), BF16), jax.ShapeDtypeStruct((B, 1, C), F32)],
        scratch_shapes=[pltpu.VMEM((tT + S, cblk), F32), pltpu.VMEM((tT, cblk), F32),
                        pltpu.VMEM((tT, cblk), F32), pltpu.VMEM((1, cblk), F32)],
        compiler_params=_params("parallel", "parallel", "arbitrary"),
        name="rglru_seq",
    )(xy, xy, h0.reshape(B, 1, C), c0, conv_w, conv_b.reshape(1, C), gate_w, gate_b, lam.reshape(1, C))
    return hg, hl.reshape(B, C)


def _rglru_step_kernel(x_ref, y_ref, h0_ref, c0_ref, cw_ref, cb_ref, gw_ref, gb_ref, lam_ref,
                       hg_ref, hl_ref):
    x = x_ref[...]
    cw = cw_ref[...]
    nw = cw.shape[0]
    conv = cb_ref[...] + cw[nw - 1:nw, :] * x
    for j in range(1, nw):
        conv = conv + cw[nw - 1 - j:nw - j, :] * c0_ref[nw - 1 - j]
    a, b = _rglru_coeffs(conv, gw_ref, gb_ref, lam_ref)
    h = a * h0_ref[...] + b
    hl_ref[...] = h
    hg_ref[...] = (h * _gelu_tanh(y_ref[...])).astype(hg_ref.dtype)


def _rglru_step(xy, h0, conv0, conv_w, conv_b, gate_w, gate_b, lam):
    B, C2 = xy.shape
    C = C2 // 2
    nblk, cblk = gate_w.shape[1], gate_w.shape[2]
    nw = conv_w.shape[0]
    c0 = jnp.swapaxes(conv0, 0, 1)
    return pl.pallas_call(
        _rglru_step_kernel,
        grid=(nblk,),
        in_specs=[
            pl.BlockSpec((B, cblk), lambda n: (0, n)),
            pl.BlockSpec((B, cblk), lambda n: (0, n + nblk)),
            pl.BlockSpec((B, cblk), lambda n: (0, n)),
            pl.BlockSpec((nw - 1, B, cblk), lambda n: (0, 0, n)),
            pl.BlockSpec((nw, cblk), lambda n: (0, n)),
            pl.BlockSpec((1, cblk), lambda n: (0, n)),
            pl.BlockSpec((2, 1, cblk, cblk), lambda n: (0, n, 0, 0)),
            pl.BlockSpec((2, cblk), lambda n: (0, n)),
            pl.BlockSpec((1, cblk), lambda n: (0, n)),
        ],
        out_specs=[pl.BlockSpec((B, cblk), lambda n: (0, n)), pl.BlockSpec((B, cblk), lambda n: (0, n))],
        out_shape=[jax.ShapeDtypeStruct((B, C), BF16), jax.ShapeDtypeStruct((B, C), F32)],
        compiler_params=_params("parallel"),
        name="rglru_step",
    )(xy, xy, h0, c0, conv_w, conv_b.reshape(1, C), gate_w, gate_b, lam.reshape(1, C))


def _tri_dots(lp, carry_col_ref, carry_row_ref, cum_col_ref, cum_row_ref):
    P = lp.shape[0]
    hi = lax.Precision.HIGHEST
    r = lax.broadcasted_iota(jnp.int32, (P, P), 0)
    c = lax.broadcasted_iota(jnp.int32, (P, P), 1)
    ones = jnp.ones((P, P), F32)
    if cum_col_ref is not None:
        tril = (c <= r).astype(F32)
        cum_col_ref[0] = jnp.dot(tril, lp, precision=hi, preferred_element_type=F32) + carry_col_ref[...]
        carry_col_ref[...] += jnp.dot(ones, lp, precision=hi, preferred_element_type=F32)
    lpt = lp.T
    triu = (r <= c).astype(F32)
    cum_row_ref[0] = jnp.dot(lpt, triu, precision=hi, preferred_element_type=F32) + carry_row_ref[...]
    carry_row_ref[...] += jnp.dot(lpt, ones, precision=hi, preferred_element_type=F32)


def _logf_cum_kernel(f_ref, bf_ref, lf_ref, cc_ref, cr_ref, car_c, car_r):
    @pl.when(pl.program_id(1) == 0)
    def _():
        car_c[...] = jnp.zeros_like(car_c)
        car_r[...] = jnp.zeros_like(car_r)

    lp = _log_sigmoid(f_ref[0] + bf_ref[...])
    lf_ref[0] = lp
    _tri_dots(lp, car_c, car_r, cc_ref, cr_ref)


def _logf_cum(f, b_f_pad):
    B, T, L = f.shape
    P = V7X_LANES
    assert T % P == 0 and L == P
    return pl.pallas_call(
        _logf_cum_kernel,
        grid=(B, T // P),
        in_specs=[pl.BlockSpec((1, P, L), lambda b, t: (b, t, 0)), pl.BlockSpec((1, L), lambda b, t: (0, 0))],
        out_specs=[pl.BlockSpec((1, P, L), lambda b, t: (b, t, 0)),
                   pl.BlockSpec((1, P, L), lambda b, t: (b, t, 0)),
                   pl.BlockSpec((1, L, P), lambda b, t: (b, 0, t))],
        out_shape=[jax.ShapeDtypeStruct((B, T, L), F32), jax.ShapeDtypeStruct((B, T, L), F32),
                   jax.ShapeDtypeStruct((B, L, T), F32)],
        scratch_shapes=[pltpu.VMEM((P, L), F32), pltpu.VMEM((L, P), F32)],
        compiler_params=_params("parallel", "arbitrary"),
        name="logf_cum",
    )(f, b_f_pad)


def _paged_cum_kernel(pt_ref, lf_ref, cr_ref, tot_ref, car_r):
    @pl.when(pl.program_id(1) == 0)
    def _():
        car_r[...] = jnp.zeros_like(car_r)

    _tri_dots(lf_ref[0], None, car_r, None, cr_ref)
    tot_ref[0] = car_r[...]


def _paged_cum(cache_logf, page_table):
    n_pool, P, H = cache_logf.shape
    DB, n_pages = page_table.shape
    assert P == V7X_LANES
    return pl.pallas_call(
        _paged_cum_kernel,
        grid_spec=pltpu.PrefetchScalarGridSpec(
            num_scalar_prefetch=1,
            grid=(DB, n_pages),
            in_specs=[pl.BlockSpec((1, P, H), lambda b, p, pt: (pt[b, p], 0, 0))],
            out_specs=[pl.BlockSpec((1, H, P), lambda b, p, pt: (b, 0, p)),
                       pl.BlockSpec((1, H, P), lambda b, p, pt: (b, 0, 0))],
            scratch_shapes=[pltpu.VMEM((H, P), F32)],
        ),
        out_shape=[jax.ShapeDtypeStruct((DB, H, n_pages * P), F32), jax.ShapeDtypeStruct((DB, H, P), F32)],
        compiler_params=_params("parallel", "arbitrary"),
        name="paged_cum",
    )(page_table, cache_logf)


def _fox_prompt_kernel(q_ref, k_ref, v_ref, cq_ref, ck_ref, o_ref, *, tq, hd, hb, scale):
    qi = pl.program_id(2)
    q_pos = qi * tq + lax.broadcasted_iota(jnp.int32, (tq, 1), 0)
    k_off = lax.broadcasted_iota(jnp.int32, (1, tq), 1)
    for j in range(hb):
        cols = slice(j * hd, (j + 1) * hd)
        qj = q_ref[0, :, cols]
        cqj = cq_ref[0, 0, :, j:j + 1]

        def chunk(kc, carry, cols=cols, qj=qj, cqj=cqj, j=j):
            m, l, acc = carry
            ks = pl.multiple_of(kc * tq, tq)
            kj = k_ref[0, pl.ds(ks, tq), cols]
            vj = v_ref[0, pl.ds(ks, tq), cols]
            s = lax.dot_general(qj, kj, (((1,), (1,)), ((), ())), preferred_element_type=F32) * scale
            s = s + (cqj - ck_ref[0, 0, j:j + 1, pl.ds(ks, tq)])
            s = jnp.where(ks + k_off <= q_pos, s, -jnp.inf)
            m_new = jnp.maximum(m, jnp.max(s, axis=1, keepdims=True))
            alpha = jnp.exp(m - m_new)
            p = jnp.exp(s - m_new)
            l = alpha * l + jnp.sum(p, axis=1, keepdims=True)
            acc = alpha * acc + jnp.dot(p.astype(BF16), vj, preferred_element_type=F32)
            return m_new, l, acc

        init = (jnp.full((tq, 1), -jnp.inf, F32), jnp.zeros((tq, 1), F32), jnp.zeros((tq, hd), F32))
        _, l, acc = lax.fori_loop(0, qi + 1, chunk, init)
        o_ref[0, :, cols] = (acc / l).astype(o_ref.dtype)


def _fox_prompt(q, k, v, cum_col, cum_row, n_heads):
    B, T, D = q.shape
    hd = D // n_heads
    hb = min(8, n_heads)
    hg = n_heads // hb
    tq = min(256, T)
    assert n_heads % hb == 0 and T % tq == 0
    cq = jnp.transpose(cum_col[:, :, :n_heads].reshape(B, T, hg, hb), (0, 2, 1, 3))
    ck = cum_row[:, :n_heads, :].reshape(B, hg, hb, T)
    return pl.pallas_call(
        functools.partial(_fox_prompt_kernel, tq=tq, hd=hd, hb=hb, scale=hd ** -0.5),
        grid=(B, hg, T // tq),
        in_specs=[pl.BlockSpec((1, tq, hb * hd), lambda b, g, i: (b, i, g)),
                  pl.BlockSpec((1, T, hb * hd), lambda b, g, i: (b, 0, g)),
                  pl.BlockSpec((1, T, hb * hd), lambda b, g, i: (b, 0, g)),
                  pl.BlockSpec((1, 1, tq, hb), lambda b, g, i: (b, g, i, 0)),
                  pl.BlockSpec((1, 1, hb, T), lambda b, g, i: (b, g, 0, 0))],
        out_specs=pl.BlockSpec((1, tq, hb * hd), lambda b, g, i: (b, i, g)),
        out_shape=jax.ShapeDtypeStruct((B, T, D), BF16),
        compiler_params=_params("parallel", "parallel", "arbitrary"),
        name="fox_prompt",
    )(q, k, v, cq, ck)


def _fox_decode_kernel(pt_ref, qbd_ref, k_ref, v_ref, ck_ref, cq_ref, qn_ref, kn_ref, vn_ref,
                       o_ref, m_ref, l_ref, acc_ref, *, hd, scale):
    p_idx = pl.program_id(1)
    H = qbd_ref.shape[1]

    @pl.when(p_idx == 0)
    def _():
        m_ref[...] = jnp.full_like(m_ref, -jnp.inf)
        l_ref[...] = jnp.zeros_like(l_ref)
        acc_ref[...] = jnp.zeros_like(acc_ref)

    kb = k_ref[0].astype(BF16)
    s = lax.dot_general(qbd_ref[0], kb, (((1,), (1,)), ((), ())), preferred_element_type=F32) * scale
    s = s + (cq_ref[0] - ck_ref[0])
    m_prev = m_ref[...]
    m_new = jnp.maximum(m_prev, jnp.max(s, axis=1, keepdims=True))
    alpha = jnp.exp(m_prev - m_new)
    p = jnp.exp(s - m_new)
    l_ref[...] = alpha * l_ref[...] + jnp.sum(p, axis=1, keepdims=True)
    m_ref[...] = m_new
    pv = jnp.dot(p.astype(BF16), v_ref[0].astype(BF16), preferred_element_type=F32)
    acc_ref[...] = acc_ref[...] * alpha[:, 0:1] + pv

    @pl.when(p_idx == pl.num_programs(1) - 1)
    def _():
        qn = qn_ref[0].astype(BF16).astype(F32)
        kn = kn_ref[0].astype(BF16).astype(F32)
        vn = vn_ref[0].astype(BF16).astype(F32)
        cq = cq_ref[0]
        s_n = jnp.sum(qn * kn, axis=1, keepdims=True) * scale + (cq - cq)
        m_p = m_ref[...]
        m_f = jnp.maximum(m_p, s_n)
        al = jnp.exp(m_p - m_f)
        p_n = jnp.exp(s_n - m_f)
        l_f = al * l_ref[...] + p_n
        head = lax.broadcasted_iota(jnp.int32, (H, hd), 0)
        diag = jnp.zeros((H, hd), F32)
        for h in range(H):
            diag = diag + jnp.where(head == h, acc_ref[:, h * hd:(h + 1) * hd], 0.0)
        o = al[:, 0:1] * diag + p_n[:, 0:1].astype(BF16).astype(F32) * vn
        o_ref[0] = (o / l_f[:, 0:1]).astype(o_ref.dtype)


def _fox_decode(q, k_new, v_new, cache_k, cache_v, page_table, cum_row_past, cum_new):
    DB, H, hd = q.shape
    n_pool, P, D = cache_k.shape
    n_pages = page_table.shape[1]
    assert P == V7X_LANES and hd == V7X_LANES
    eye = jnp.eye(H, dtype=F32)
    qbd = (q[:, :, None, :] * eye[None, :, :, None]).reshape(DB, H, D).astype(BF16)
    per_seq = lambda shape: pl.BlockSpec((1,) + shape, lambda b, p, pt: (b, 0, 0))
    return pl.pallas_call(
        functools.partial(_fox_decode_kernel, hd=hd, scale=hd ** -0.5),
        grid_spec=pltpu.PrefetchScalarGridSpec(
            num_scalar_prefetch=1,
            grid=(DB, n_pages),
            in_specs=[per_seq((H, D)),
                      pl.BlockSpec((1, P, D), lambda b, p, pt: (pt[b, p], 0, 0)),
                      pl.BlockSpec((1, P, D), lambda b, p, pt: (pt[b, p], 0, 0)),
                      pl.BlockSpec((1, H, P), lambda b, p, pt: (b, 0, p)),
                      per_seq((H, P)), per_seq((H, hd)), per_seq((H, hd)), per_seq((H, hd))],
            out_specs=per_seq((H, hd)),
            scratch_shapes=[pltpu.VMEM((H, P), F32), pltpu.VMEM((H, P), F32), pltpu.VMEM((H, D), F32)],
        ),
        out_shape=jax.ShapeDtypeStruct((DB, H, hd), BF16),
        compiler_params=_params("parallel", "arbitrary"),
        name="fox_decode",
    )(page_table, qbd, cache_k, cache_v, cum_row_past, cum_new, q, k_new, v_new)


def _run_group(x3, h0, conv0, past, p):
    B, T, D = x3.shape
    M = B * T
    depth = p["w_up"].shape[0]
    n_a = p["w_in"].shape[0]
    H = p["b_f"].shape[0]
    hd = p["w_q"].shape[2] // H
    d_attn = H * hd
    L = V7X_LANES
    nw = p["conv_w"].shape[1]

    x = x3.reshape(M, D)
    xn = _norm_cast(x, p["norm_mix_pre"][0])
    hs, convs = [], []
    kv = None
    for l in range(depth):
        if l < n_a:
            xy = _mm_fullk(xn, p["w_in"][l], epilogue="f32")
            C = xy.shape[1] // 2
            args = (p["conv_w"][l], p["conv_b"][l], p["gate_w"][l], p["gate_b"][l], p["rg_lambda"][l])
            if T == 1:
                hg, h_l = _rglru_step(xy, h0[l], conv0[l], *args)
                c_l = jnp.concatenate([conv0[l], xy[:, None, :C]], axis=1)[:, 1:]
            else:
                hg, h_l = _rglru_seq(xy.reshape(B, T, 2 * C), h0[l], conv0[l], *args)
                hg = hg.reshape(M, C)
                xb_tail = xy.reshape(B, T, 2 * C)[:, T - (nw - 1):, :C]
                c_l = xb_tail if T >= nw - 1 else jnp.concatenate([conv0[l], xb_tail], axis=1)[:, -(nw - 1):]
            hs.append(h_l)
            convs.append(c_l)
            m = _mm_fullk(hg, p["w_out_a"][l], epilogue="f32")
        else:
            j = l - n_a
            q = _mm_fullk(xn, p["w_q"][j], epilogue="f32" if T == 1 else "bf16")
            if T == 1:
                o = _fox_decode(q.reshape(B, H, hd), kv["k"].reshape(B, H, hd), kv["v"].reshape(B, H, hd),
                                kv["cache_k"], kv["cache_v"], kv["page_table"], kv["cum_row_past"], kv["cum_new"])
            else:
                o = _fox_prompt(q.reshape(B, T, d_attn), kv["kb"].reshape(B, T, d_attn),
                                kv["vb"].reshape(B, T, d_attn), kv["cum_col"], kv["cum_row"], H)
            m = _mm_fullk(o.reshape(M, d_attn), p["w_o"][j], epilogue="f32")
        x, (xn2,) = _resid_norm(x, m, p["norm_mix_post"][l], [p["norm_mlp_pre"][l]])
        hmid = _mm_fullk(xn2, p["w_up"][l], epilogue="relu2_bf16")
        f = _mm_ktiled(hmid, p["w_down"][l])
        if l + 1 == n_a:
            x, (xkv, xn) = _resid_norm(x, f, p["norm_mlp_post"][l], [p["norm_kv"], p["norm_mix_pre"][l + 1]])
            kv = _shared_kv(xkv, B, T, H, hd, past, p)
        elif l + 1 < depth:
            x, (xn,) = _resid_norm(x, f, p["norm_mlp_post"][l], [p["norm_mix_pre"][l + 1]])
        else:
            x, _ = _resid_norm(x, f, p["norm_mlp_post"][l], [])
    return (x.reshape(B, T, D), jnp.stack(hs), jnp.stack(convs),
            kv["k"].reshape(B, T, H, hd), kv["v"].reshape(B, T, H, hd), kv["logf"])


def _shared_kv(xkv, B, T, H, hd, past, p):
    d_attn = H * hd
    L = V7X_LANES
    w_kvf = p["w_kvf"]
    tn = min(512, d_attn)
    k, kb = _mm_fullk(xkv, w_kvf, n_cols=d_attn, col_block0=0, epilogue="f32_bf16")
    v, vb = _mm_fullk(xkv, w_kvf, n_cols=d_attn, col_block0=d_attn // tn, epilogue="f32_bf16")
    w_f = jnp.pad(w_kvf[:, 2 * d_attn:], ((0, 0), (0, L - H)))
    b_f = jnp.pad(p["b_f"], (0, L - H)).reshape(1, L)
    f = _mm_fullk(xkv, w_f, epilogue="f32")
    kv = dict(k=k, v=v, kb=kb, vb=vb)
    if past is None:
        logf, cum_col, cum_row = _logf_cum(f.reshape(B, T, L), b_f)
        kv.update(logf=logf[:, :, :H], cum_col=cum_col, cum_row=cum_row)
    else:
        cache_k, cache_v, cache_logf, page_table = past
        n_pool, P = cache_k.shape[0], cache_k.shape[1]
        cum_row_past, tot = _paged_cum(cache_logf, page_table)
        logf, cum_new = _logf_new(f, b_f, tot)
        kv.update(logf=logf[:, :H].reshape(B, T, H), cum_row_past=cum_row_past, cum_new=cum_new,
                  cache_k=cache_k.reshape(n_pool, P, d_attn), cache_v=cache_v.reshape(n_pool, P, d_attn),
                  page_table=page_table)
    return kv


def _logf_new_kernel(f_ref, bf_ref, tot_ref, lf_ref, cn_ref):
    lp = _log_sigmoid(f_ref[...] + bf_ref[...])
    lf_ref[...] = lp
    H = tot_ref.shape[1]
    on_diag = (lax.broadcasted_iota(jnp.int32, (H, lp.shape[1]), 0)
               == lax.broadcasted_iota(jnp.int32, (H, lp.shape[1]), 1))
    for b in range(tot_ref.shape[0]):
        col = jnp.sum(jnp.where(on_diag, lp[b:b + 1, :], 0.0), axis=1, keepdims=True)
        cn_ref[b] = tot_ref[b] + col


def _logf_new(f, b_f_pad, tot):
    DB, L = f.shape
    _, H, P = tot.shape
    return pl.pallas_call(
        _logf_new_kernel,
        out_shape=[jax.ShapeDtypeStruct((DB, L), F32), jax.ShapeDtypeStruct((DB, H, P), F32)],
        compiler_params=pltpu.CompilerParams(vmem_limit_bytes=VMEM_LIMIT),
        name="logf_new",
    )(f, b_f_pad, tot)


def kernel(x_prompt, x_sample, state_rglru_h, state_conv, cache_k, cache_v, cache_logf, page_table,
           w_in, conv_w, conv_b, gate_w, gate_b, rg_lambda, w_out_a, norm_kv, w_kvf, b_f, w_q, w_o,
           norm_mix_pre, norm_mix_post, norm_mlp_pre, norm_mlp_post, w_up, w_down):
    p = dict(w_in=w_in, conv_w=conv_w, conv_b=conv_b, gate_w=gate_w, gate_b=gate_b, rg_lambda=rg_lambda,
             w_out_a=w_out_a, norm_kv=norm_kv, w_kvf=w_kvf, b_f=b_f, w_q=w_q, w_o=w_o,
             norm_mix_pre=norm_mix_pre, norm_mix_post=norm_mix_post, norm_mlp_pre=norm_mlp_pre,
             norm_mlp_post=norm_mlp_post, w_up=w_up, w_down=w_down)
    n_a, _, d_rnn = state_rglru_h.shape
    nw = conv_w.shape[1]
    B = x_prompt.shape[0]
    h0 = jnp.zeros((n_a, B, d_rnn), F32)
    c0 = jnp.zeros((n_a, B, nw - 1, d_rnn), F32)
    out_p = _run_group(x_prompt, h0, c0, None, p)
    out_s = _run_group(x_sample, state_rglru_h, state_conv, (cache_k, cache_v, cache_logf, page_table), p)
    return (out_p[0], out_s[0]) + out_p[1:] + out_s[1:]
```

```python
import functools
import math

import jax
import jax.numpy as jnp
from jax import lax
from jax.experimental import pallas as pl
from jax.experimental.pallas import tpu as pltpu

F32 = jnp.float32
BF16 = jnp.bfloat16

RMS_EPS = 1e-6
LRU_C = 8.0

V7X_VMEM_BYTES = 64 * 1024 * 1024
V7X_LANES = 128
V7X_SUBLANES = 8
VMEM_LIMIT = 56 * 1024 * 1024


def _params(*sem):
    return pltpu.CompilerParams(dimension_semantics=sem, vmem_limit_bytes=VMEM_LIMIT)


def _rms_scale(x):
    return x * lax.rsqrt(jnp.mean(x * x, axis=-1, keepdims=True) + RMS_EPS)


def _log_sigmoid(x):
    return jnp.minimum(x, 0.0) - jnp.log1p(jnp.exp(-jnp.abs(x)))


def _gelu_tanh(x):
    c = math.sqrt(2.0 / math.pi)
    return 0.5 * x * (1.0 + jnp.tanh(c * (x + 0.044715 * (x * x * x))))


def _norm_cast_kernel(x_ref, g_ref, o_ref):
    o_ref[...] = (_rms_scale(x_ref[...]) * g_ref[...]).astype(o_ref.dtype)


def _norm_cast(x, g):
    M, D = x.shape
    tr = min(256, M)
    return pl.pallas_call(
        _norm_cast_kernel,
        grid=(M // tr,),
        in_specs=[pl.BlockSpec((tr, D), lambda i: (i, 0)), pl.BlockSpec((1, D), lambda i: (0, 0))],
        out_specs=pl.BlockSpec((tr, D), lambda i: (i, 0)),
        out_shape=jax.ShapeDtypeStruct((M, D), BF16),
        compiler_params=_params("parallel"),
        name="norm_cast",
    )(x, g.reshape(1, D))


def _resid_norm_kernel(x_ref, m_ref, gp_ref, gn_ref, xo_ref, *n_refs):
    xn = x_ref[...] + _rms_scale(m_ref[...]) * gp_ref[...]
    xo_ref[...] = xn
    if n_refs:
        y = _rms_scale(xn)
        for j, r in enumerate(n_refs):
            r[...] = (y * gn_ref[j:j + 1, :]).astype(r.dtype)


def _resid_norm(x, m, g_post, g_next):
    M, D = x.shape
    n = len(g_next)
    tr = min(256, M)
    gn = jnp.stack(g_next) if n else jnp.zeros((1, D), F32)
    row = pl.BlockSpec((tr, D), lambda i: (i, 0))
    outs = pl.pallas_call(
        _resid_norm_kernel,
        grid=(M // tr,),
        in_specs=[row, row, pl.BlockSpec((1, D), lambda i: (0, 0)),
                  pl.BlockSpec((gn.shape[0], D), lambda i: (0, 0))],
        out_specs=[row] * (1 + n),
        out_shape=[jax.ShapeDtypeStruct((M, D), F32)] + [jax.ShapeDtypeStruct((M, D), BF16)] * n,
        compiler_params=_params("parallel"),
        name="resid_norm",
    )(x, m, g_post.reshape(1, D), gn)
    return outs[0], list(outs[1:])


def _mm_store(acc, o_refs, epilogue):
    if epilogue == "f32":
        o_refs[0][...] = acc
    elif epilogue == "bf16":
        o_refs[0][...] = acc.astype(BF16)
    elif epilogue == "f32_bf16":
        o_refs[0][...] = acc
        o_refs[1][...] = acc.astype(BF16)
    elif epilogue == "relu2_bf16":
        h = jnp.maximum(acc, 0.0)
        o_refs[0][...] = (h * h).astype(BF16)
    else:
        raise ValueError(epilogue)


def _mm_fullk_kernel(a_ref, w_ref, *rest, epilogue):
    *o_refs, wb_ref = rest

    @pl.when(pl.program_id(1) == 0)
    def _():
        wb_ref[...] = w_ref[...].astype(BF16)

    acc = jnp.dot(a_ref[...], wb_ref[...], preferred_element_type=F32)
    _mm_store(acc, o_refs, epilogue)


_EPILOGUE_DTYPES = {"f32": (F32,), "bf16": (BF16,), "f32_bf16": (F32, BF16), "relu2_bf16": (BF16,)}


def _mm_fullk(a, w, layer=0, *, n_cols=None, col_block0=0, epilogue="f32"):
    if w.ndim == 2:
        w = w[None]
    M, K = a.shape
    N = w.shape[2] if n_cols is None else n_cols
    tm = min(1024, M)
    tn = min(512, N)
    assert M % tm == 0 and N % tn == 0
    dts = _EPILOGUE_DTYPES[epilogue]
    out_spec = pl.BlockSpec((tm, tn), lambda n, m: (m, n))
    outs = pl.pallas_call(
        functools.partial(_mm_fullk_kernel, epilogue=epilogue),
        grid=(N // tn, M // tm),
        in_specs=[pl.BlockSpec((tm, K), lambda n, m: (m, 0)),
                  pl.BlockSpec((None, K, tn), lambda n, m: (layer, 0, n + col_block0))],
        out_specs=[out_spec] * len(dts),
        out_shape=[jax.ShapeDtypeStruct((M, N), d) for d in dts],
        scratch_shapes=[pltpu.VMEM((K, tn), BF16)],
        compiler_params=_params("arbitrary", "arbitrary"),
        name="mm_fullk_" + epilogue,
    )(a, w)
    return outs if len(dts) > 1 else outs[0]


def _mm_ktiled_kernel(a_ref, w_ref, o_ref):
    part = jnp.dot(a_ref[...], w_ref[...].astype(BF16), preferred_element_type=F32)

    @pl.when(pl.program_id(2) == 0)
    def _():
        o_ref[...] = part

    @pl.when(pl.program_id(2) != 0)
    def _():
        o_ref[...] += part


def _mm_ktiled(a, w, layer):
    M, K = a.shape
    N = w.shape[2]
    tm = min(1024, M)
    tn = min(1024, N)
    tk = min(2048, K)
    assert M % tm == 0 and N % tn == 0 and K % tk == 0
    return pl.pallas_call(
        _mm_ktiled_kernel,
        grid=(M // tm, N // tn, K // tk),
        in_specs=[pl.BlockSpec((tm, tk), lambda m, n, k: (m, k)),
                  pl.BlockSpec((None, tk, tn), lambda m, n, k: (layer, k, n))],
        out_specs=pl.BlockSpec((tm, tn), lambda m, n, k: (m, n)),
        out_shape=jax.ShapeDtypeStruct((M, N), F32),
        compiler_params=_params("parallel", "parallel", "arbitrary"),
        name="mm_ktiled",
    )(a, w)


def _rglru_coeffs(conv, gw_ref, gb_ref, lam_ref):
    cb = conv.astype(BF16)
    gi = jnp.dot(cb, gw_ref[0, 0].astype(BF16), preferred_element_type=F32) + gb_ref[0:1, :]
    gr = jnp.dot(cb, gw_ref[1, 0].astype(BF16), preferred_element_type=F32) + gb_ref[1:2, :]
    i_gate = jax.nn.sigmoid(gi)
    r_gate = jax.nn.sigmoid(gr)
    log_a = LRU_C * r_gate * _log_sigmoid(lam_ref[...])
    a = jnp.exp(log_a)
    th = jnp.tanh(log_a)
    mult = jnp.sqrt(-2.0 * th / (1.0 - th))
    return a, mult * i_gate * conv


def _rglru_seq_kernel(x_ref, y_ref, h0_ref, c0_ref, cw_ref, cb_ref, gw_ref, gb_ref, lam_ref,
                      hg_ref, hl_ref, xs_ref, a_ref, b_ref, hc_ref, *, tT):
    t = pl.program_id(2)
    S = V7X_SUBLANES

    @pl.when(t == 0)
    def _():
        xs_ref[0:S, :] = c0_ref[0]
        hc_ref[...] = h0_ref[0]

    x = x_ref[0]
    xs_ref[S:S + tT, :] = x
    cw = cw_ref[...]
    nw = cw.shape[0]
    conv = cb_ref[...] + cw[nw - 1:nw, :] * x
    for j in range(1, nw):
        conv = conv + cw[nw - 1 - j:nw - j, :] * xs_ref[S - j:S - j + tT, :]
    xs_ref[0:S, :] = xs_ref[tT:tT + S, :]

    a, b = _rglru_coeffs(conv, gw_ref, gb_ref, lam_ref)
    a_ref[...] = a
    b_ref[...] = b

    row = lax.broadcasted_iota(jnp.int32, (S, a.shape[1]), 0)

    def group(i, h):
        r = pl.multiple_of(i * S, S)
        a8 = a_ref[pl.ds(r, S), :]
        b8 = b_ref[pl.ds(r, S), :]
        s = 1
        while s < S:
            keep = row >= s
            b8 = jnp.where(keep, a8 * pltpu.roll(b8, s, 0) + b8, b8)
            a8 = jnp.where(keep, a8 * pltpu.roll(a8, s, 0), a8)
            s *= 2
        h8 = a8 * h + b8
        b_ref[pl.ds(r, S), :] = h8
        return h8[S - 1:S, :]

    h_last = lax.fori_loop(0, tT // S, group, hc_ref[...])
    hc_ref[...] = h_last
    hl_ref[0] = h_last
    hg_ref[0] = (b_ref[...] * _gelu_tanh(y_ref[0])).astype(hg_ref.dtype)


def _rglru_seq(xy, h0, conv0, conv_w, conv_b, gate_w, gate_b, lam):
    B, T, C2 = xy.shape
    C = C2 // 2
    nblk, cblk = gate_w.shape[1], gate_w.shape[2]
    nw = conv_w.shape[0]
    S = V7X_SUBLANES
    tT = min(512, T)
    assert T % tT == 0 and tT % S == 0 and nw - 1 <= S and cblk * nblk == C
    c0 = jnp.concatenate([jnp.zeros((B, S - (nw - 1), C), F32), conv0], axis=1)
    hg, hl = pl.pallas_call(
        functools.partial(_rglru_seq_kernel, tT=tT),
        grid=(B, nblk, T // tT),
        in_specs=[
            pl.BlockSpec((1, tT, cblk), lambda b, n, t: (b, t, n)),
            pl.BlockSpec((1, tT, cblk), lambda b, n, t: (b, t, n + nblk)),
            pl.BlockSpec((1, 1, cblk), lambda b, n, t: (b, 0, n)),
            pl.BlockSpec((1, S, cblk), lambda b, n, t: (b, 0, n)),
            pl.BlockSpec((nw, cblk), lambda b, n, t: (0, n)),
            pl.BlockSpec((1, cblk), lambda b, n, t: (0, n)),
            pl.BlockSpec((2, 1, cblk, cblk), lambda b, n, t: (0, n, 0, 0)),
            pl.BlockSpec((2, cblk), lambda b, n, t: (0, n)),
            pl.BlockSpec((1, cblk), lambda b, n, t: (0, n)),
        ],
        out_specs=[pl.BlockSpec((1, tT, cblk), lambda b, n, t: (b, t, n)),
                   pl.BlockSpec((1, 1, cblk), lambda b, n, t: (b, 0, n))],
        out_shape=[jax.ShapeDtypeStruct((B, T, C), BF16), jax.ShapeDtypeStruct((B, 1, C), F32)],
        scratch_shapes=[pltpu.VMEM((tT + S, cblk), F32), pltpu.VMEM((tT, cblk), F32),
                        pltpu.VMEM((tT, cblk), F32), pltpu.VMEM((1, cblk), F32)],
        compiler_params=_params("parallel", "parallel", "arbitrary"),
        name="rglru_seq",
    )(xy, xy, h0.reshape(B, 1, C), c0, conv_w, conv_b.reshape(1, C), gate_w, gate_b, lam.reshape(1, C))
    return hg, hl.reshape(B, C)


def _rglru_step_kernel(x_ref, y_ref, h0_ref, c0_ref, cw_ref, cb_ref, gw_ref, gb_ref, lam_ref,
                       hg_ref, hl_ref):
    x = x_ref[...]
    cw = cw_ref[...]
    nw = cw.shape[0]
    conv = cb_ref[...] + cw[nw - 1:nw, :] * x
    for j in range(1, nw):
        conv = conv + cw[nw - 1 - j:nw - j, :] * c0_ref[nw - 1 - j]
    a, b = _rglru_coeffs(conv, gw_ref, gb_ref, lam_ref)
    h = a * h0_ref[...] + b
    hl_ref[...] = h
    hg_ref[...] = (h * _gelu_tanh(y_ref[...])).astype(hg_ref.dtype)


def _rglru_step(xy, h0, conv0, conv_w, conv_b, gate_w, gate_b, lam):
    B, C2 = xy.shape
    C = C2 // 2
    nblk, cblk = gate_w.shape[1], gate_w.shape[2]
    nw = conv_w.shape[0]
    c0 = jnp.swapaxes(conv0, 0, 1)
    return pl.pallas_call(
        _rglru_step_kernel,
        grid=(nblk,),
        in_specs=[
            pl.BlockSpec((B, cblk), lambda n: (0, n)),
            pl.BlockSpec((B, cblk), lambda n: (0, n + nblk)),
            pl.BlockSpec((B, cblk), lambda n: (0, n)),
            pl.BlockSpec((nw - 1, B, cblk), lambda n: (0, 0, n)),
            pl.BlockSpec((nw, cblk), lambda n: (0, n)),
            pl.BlockSpec((1, cblk), lambda n: (0, n)),
            pl.BlockSpec((2, 1, cblk, cblk), lambda n: (0, n, 0, 0)),
            pl.BlockSpec((2, cblk), lambda n: (0, n)),
            pl.BlockSpec((1, cblk), lambda n: (0, n)),
        ],
        out_specs=[pl.BlockSpec((B, cblk), lambda n: (0, n)), pl.BlockSpec((B, cblk), lambda n: (0, n))],
        out_shape=[jax.ShapeDtypeStruct((B, C), BF16), jax.ShapeDtypeStruct((B, C), F32)],
        compiler_params=_params("parallel"),
        name="rglru_step",
    )(xy, xy, h0, c0, conv_w, conv_b.reshape(1, C), gate_w, gate_b, lam.reshape(1, C))


def _tri_dots(lp, carry_col_ref, carry_row_ref, cum_col_ref, cum_row_ref):
    P = lp.shape[0]
    hi = lax.Precision.HIGHEST
    r = lax.broadcasted_iota(jnp.int32, (P, P), 0)
    c = lax.broadcasted_iota(jnp.int32, (P, P), 1)
    ones = jnp.ones((P, P), F32)
    if cum_col_ref is not None:
        tril = (c <= r).astype(F32)
        cum_col_ref[0] = jnp.dot(tril, lp, precision=hi, preferred_element_type=F32) + carry_col_ref[...]
        carry_col_ref[...] += jnp.dot(ones, lp, precision=hi, preferred_element_type=F32)
    lpt = lp.T
    triu = (r <= c).astype(F32)
    cum_row_ref[0] = jnp.dot(lpt, triu, precision=hi, preferred_element_type=F32) + carry_row_ref[...]
    carry_row_ref[...] += jnp.dot(lpt, ones, precision=hi, preferred_element_type=F32)


def _logf_cum_kernel(f_ref, bf_ref, lf_ref, cc_ref, cr_ref, car_c, car_r):
    @pl.when(pl.program_id(1) == 0)
    def _():
        car_c[...] = jnp.zeros_like(car_c)
        car_r[...] = jnp.zeros_like(car_r)

    lp = _log_sigmoid(f_ref[0] + bf_ref[...])
    lf_ref[0] = lp
    _tri_dots(lp, car_c, car_r, cc_ref, cr_ref)


def _logf_cum(f, b_f_pad):
    B, T, L = f.shape
    P = V7X_LANES
    assert T % P == 0 and L == P
    return pl.pallas_call(
        _logf_cum_kernel,
        grid=(B, T // P),
        in_specs=[pl.BlockSpec((1, P, L), lambda b, t: (b, t, 0)), pl.BlockSpec((1, L), lambda b, t: (0, 0))],
        out_specs=[pl.BlockSpec((1, P, L), lambda b, t: (b, t, 0)),
                   pl.BlockSpec((1, P, L), lambda b, t: (b, t, 0)),
                   pl.BlockSpec((1, L, P), lambda b, t: (b, 0, t))],
        out_shape=[jax.ShapeDtypeStruct((B, T, L), F32), jax.ShapeDtypeStruct((B, T, L), F32),
                   jax.ShapeDtypeStruct((B, L, T), F32)],
        scratch_shapes=[pltpu.VMEM((P, L), F32), pltpu.VMEM((L, P), F32)],
        compiler_params=_params("parallel", "arbitrary"),
        name="logf_cum",
    )(f, b_f_pad)


def _paged_cum_kernel(pt_ref, *refs, n_in):
    lf_refs = refs[:n_in]
    cc_ref, tot_ref, car = refs[n_in:]

    @pl.when(pl.program_id(1) == 0)
    def _():
        car[...] = jnp.zeros_like(car)

    P = lf_refs[0].shape[1]
    hi = lax.Precision.HIGHEST
    tril = (lax.broadcasted_iota(jnp.int32, (P, P), 1) <= lax.broadcasted_iota(jnp.int32, (P, P), 0)).astype(F32)
    ones = jnp.ones((P, P), F32)
    for g, lf_ref in enumerate(lf_refs):
        lp = lf_ref[0]
        cc_ref[0, g * P:(g + 1) * P, :] = jnp.dot(tril, lp, precision=hi, preferred_element_type=F32) + car[...]
        car[...] += jnp.dot(ones, lp, precision=hi, preferred_element_type=F32)
    tot_ref[0] = car[0:V7X_SUBLANES, :]


def _paged_cum(cache_logf, page_table):
    n_pool, P, H = cache_logf.shape
    DB, n_pages = page_table.shape
    G = math.gcd(n_pages, 8)
    S = V7X_SUBLANES
    specs = [pl.BlockSpec((1, P, H), lambda b, s, pt, g=g: (pt[b, s * G + g], 0, 0)) for g in range(G)]
    cum, tot = pl.pallas_call(
        functools.partial(_paged_cum_kernel, n_in=G),
        grid_spec=pltpu.PrefetchScalarGridSpec(
            num_scalar_prefetch=1,
            grid=(DB, n_pages // G),
            in_specs=specs,
            out_specs=[pl.BlockSpec((1, G * P, H), lambda b, s, pt: (b, s, 0)),
                       pl.BlockSpec((1, S, H), lambda b, s, pt: (b, 0, 0))],
            scratch_shapes=[pltpu.VMEM((P, H), F32)],
        ),
        out_shape=[jax.ShapeDtypeStruct((DB, n_pages * P, H), F32), jax.ShapeDtypeStruct((DB, S, H), F32)],
        compiler_params=_params("parallel", "arbitrary"),
        name="paged_cum",
    )(page_table, *([cache_logf] * G))
    return cum, tot[:, 0, :]


def _fox_prompt_kernel(q_ref, k_ref, vt_ref, cq_ref, ck_ref, o_ref, *, tq, hd, hb, nq, scale):
    qi = pl.program_id(2)
    nt = (((1,), (1,)), ((), ()))
    below_or_on = (lax.broadcasted_iota(jnp.int32, (tq, 1), 0)
                   <= lax.broadcasted_iota(jnp.int32, (1, tq), 1))

    def q_block(c):
        lo = c * tq
        for j in range(hb):
            cols = slice(j * hd, (j + 1) * hd)
            qj = q_ref[0, :, cols]
            cqj = cq_ref[0, 0, j:j + 1, :]
            sd = lax.dot_general(k_ref[0, lo:lo + tq, cols], qj, nt, preferred_element_type=F32) * scale
            sd = sd + (cqj - ck_ref[0, 0, lo:lo + tq, j:j + 1])
            sd = jnp.where(below_or_on, sd, -jnp.inf)
            m = jnp.max(sd, axis=0, keepdims=True)
            if c:
                sp = lax.dot_general(k_ref[0, 0:lo, cols], qj, nt, preferred_element_type=F32) * scale
                sp = sp + (cqj - ck_ref[0, 0, 0:lo, j:j + 1])
                m = jnp.maximum(m, jnp.max(sp, axis=0, keepdims=True))
            pd = jnp.exp(sd - m)
            l = jnp.sum(pd, axis=0, keepdims=True)
            acc = jnp.dot(vt_ref[0, cols, lo:lo + tq], pd.astype(BF16), preferred_element_type=F32)
            if c:
                pp = jnp.exp(sp - m)
                l = l + jnp.sum(pp, axis=0, keepdims=True)
                acc = acc + jnp.dot(vt_ref[0, cols, 0:lo], pp.astype(BF16), preferred_element_type=F32)
            o_ref[0, :, cols] = (acc / l).T.astype(o_ref.dtype)

    for c in range(nq):
        pl.when(qi == c)(functools.partial(q_block, c))


def _fox_prompt(q, k, vt, cum_col, cum_row, n_heads):
    B, T, D = q.shape
    hd = D // n_heads
    hb = min(4, n_heads)
    hg = n_heads // hb
    tq = min(256, T)
    assert n_heads % hb == 0 and T % tq == 0
    cq = cum_row[:, :n_heads, :].reshape(B, hg, hb, T)
    ck = jnp.transpose(cum_col[:, :, :n_heads].reshape(B, T, hg, hb), (0, 2, 1, 3))
    return pl.pallas_call(
        functools.partial(_fox_prompt_kernel, tq=tq, hd=hd, hb=hb, nq=T // tq, scale=hd ** -0.5),
        grid=(B, hg, T // tq),
        in_specs=[pl.BlockSpec((1, tq, hb * hd), lambda b, g, i: (b, i, g)),
                  pl.BlockSpec((1, T, hb * hd), lambda b, g, i: (b, 0, g)),
                  pl.BlockSpec((1, hb * hd, T), lambda b, g, i: (b, g, 0)),
                  pl.BlockSpec((1, 1, hb, tq), lambda b, g, i: (b, g, 0, i)),
                  pl.BlockSpec((1, 1, T, hb), lambda b, g, i: (b, g, 0, 0))],
        out_specs=pl.BlockSpec((1, tq, hb * hd), lambda b, g, i: (b, i, g)),
        out_shape=jax.ShapeDtypeStruct((B, T, D), BF16),
        compiler_params=_params("parallel", "parallel", "arbitrary"),
        name="fox_prompt",
    )(q, k, vt, cq, ck)


def _fox_decode_kernel(pt_ref, q_ref, k_ref, v_ref, ckf_ref, cqf_ref, cqb_ref, kn_ref, vn_ref,
                       o_ref, m_ref, l_ref, acc_ref, *, scale):
    p_idx = pl.program_id(1)
    H = q_ref.shape[1]

    @pl.when(p_idx == 0)
    def _():
        m_ref[...] = jnp.full_like(m_ref, -jnp.inf)
        l_ref[...] = jnp.zeros_like(l_ref)
        acc_ref[...] = jnp.zeros_like(acc_ref)

    qb = q_ref[0].astype(BF16)
    kb = k_ref[0].astype(BF16)
    s = lax.dot_general(qb, kb, (((1,), (1,)), ((), ())), preferred_element_type=F32) * scale
    s = s + (cqf_ref[0] - ckf_ref[0, 0])
    own = (lax.broadcasted_iota(jnp.int32, s.shape, 1) % H) == lax.broadcasted_iota(jnp.int32, s.shape, 0)
    s = jnp.where(own, s, -jnp.inf)
    m_prev = m_ref[...]
    m_new = jnp.maximum(m_prev, jnp.max(s, axis=1, keepdims=True))
    alpha = jnp.exp(m_prev - m_new)
    p = jnp.exp(s - m_new[:, 0:1])
    l_ref[...] = alpha * l_ref[...] + jnp.sum(p, axis=1, keepdims=True)
    m_ref[...] = m_new
    acc_ref[...] = alpha * acc_ref[...] + jnp.dot(p.astype(BF16), v_ref[0].astype(BF16),
                                                  preferred_element_type=F32)

    @pl.when(p_idx == pl.num_programs(1) - 1)
    def _():
        qn = q_ref[0].astype(BF16).astype(F32)
        kn = kn_ref[0].astype(BF16).astype(F32)
        vn = vn_ref[0].astype(BF16).astype(F32)
        cq = cqb_ref[0]
        s_n = jnp.sum(qn * kn, axis=1, keepdims=True) * scale + (cq - cq)
        m_p = m_ref[...]
        m_f = jnp.maximum(m_p, s_n)
        al = jnp.exp(m_p - m_f)
        p_n = jnp.exp(s_n - m_f)
        l_f = al * l_ref[...] + p_n
        o = al * acc_ref[...] + p_n.astype(BF16).astype(F32) * vn
        o_ref[0] = (o / l_f).astype(o_ref.dtype)


def _fox_decode(q, k_new, v_new, cache_k, cache_v, page_table, cum_past, cum_new):
    DB, H, hd = q.shape
    n_pool, P = cache_k.shape[0], cache_k.shape[1]
    n_pages = page_table.shape[1]
    R = P * H
    assert hd == V7X_LANES
    k2 = cache_k.reshape(n_pool, R, hd)
    v2 = cache_v.reshape(n_pool, R, hd)
    ckf = cum_past.reshape(DB, n_pages, 1, R)
    cqf = jnp.tile(cum_new, (1, P)).reshape(DB, 1, R)
    cqb = jnp.broadcast_to(cum_new[:, :, None], (DB, H, hd))
    per_seq = lambda shape: pl.BlockSpec((1,) + shape, lambda b, p, pt: (b, 0, 0))
    return pl.pallas_call(
        functools.partial(_fox_decode_kernel, scale=hd ** -0.5),
        grid_spec=pltpu.PrefetchScalarGridSpec(
            num_scalar_prefetch=1,
            grid=(DB, n_pages),
            in_specs=[per_seq((H, hd)),
                      pl.BlockSpec((1, R, hd), lambda b, p, pt: (pt[b, p], 0, 0)),
                      pl.BlockSpec((1, R, hd), lambda b, p, pt: (pt[b, p], 0, 0)),
                      pl.BlockSpec((1, 1, 1, R), lambda b, p, pt: (b, p, 0, 0)),
                      per_seq((1, R)), per_seq((H, hd)), per_seq((H, hd)), per_seq((H, hd))],
            out_specs=per_seq((H, hd)),
            scratch_shapes=[pltpu.VMEM((H, hd), F32), pltpu.VMEM((H, hd), F32), pltpu.VMEM((H, hd), F32)],
        ),
        out_shape=jax.ShapeDtypeStruct((DB, H, hd), BF16),
        compiler_params=_params("parallel", "arbitrary"),
        name="fox_decode",
    )(page_table, q, k2, v2, ckf, cqf, cqb, k_new, v_new)


def _run_group(x3, h0, conv0, past, p):
    B, T, D = x3.shape
    M = B * T
    depth = p["w_up"].shape[0]
    n_a = p["w_in"].shape[0]
    H = p["b_f"].shape[0]
    hd = p["w_q"].shape[2] // H
    d_attn = H * hd
    L = V7X_LANES
    nw = p["conv_w"].shape[1]

    x = x3.reshape(M, D)
    xn = _norm_cast(x, p["norm_mix_pre"][0])
    hs, convs = [], []
    kv = None
    for l in range(depth):
        if l < n_a:
            xy = _mm_fullk(xn, p["w_in"], l, epilogue="f32")
            C = xy.shape[1] // 2
            args = (p["conv_w"][l], p["conv_b"][l], p["gate_w"][l], p["gate_b"][l], p["rg_lambda"][l])
            if T == 1:
                hg, h_l = _rglru_step(xy, h0[l], conv0[l], *args)
                c_l = jnp.concatenate([conv0[l], xy[:, None, :C]], axis=1)[:, 1:]
            else:
                hg, h_l = _rglru_seq(xy.reshape(B, T, 2 * C), h0[l], conv0[l], *args)
                hg = hg.reshape(M, C)
                xb_tail = xy.reshape(B, T, 2 * C)[:, T - (nw - 1):, :C]
                c_l = xb_tail if T >= nw - 1 else jnp.concatenate([conv0[l], xb_tail], axis=1)[:, -(nw - 1):]
            hs.append(h_l)
            convs.append(c_l)
            m = _mm_fullk(hg, p["w_out_a"], l, epilogue="f32")
        else:
            j = l - n_a
            q = _mm_fullk(xn, p["w_q"], j, epilogue="f32" if T == 1 else "bf16")
            if T == 1:
                o = _fox_decode(q.reshape(B, H, hd), kv["k"].reshape(B, H, hd), kv["v"].reshape(B, H, hd),
                                kv["cache_k"], kv["cache_v"], kv["page_table"], kv["cum_past"], kv["cum_new"])
            else:
                o = _fox_prompt(q.reshape(B, T, d_attn), kv["kb"].reshape(B, T, d_attn), kv["vt"],
                                kv["cum_col"], kv["cum_row"], H)
            m = _mm_fullk(o.reshape(M, d_attn), p["w_o"], j, epilogue="f32")
        x, (xn2,) = _resid_norm(x, m, p["norm_mix_post"][l], [p["norm_mlp_pre"][l]])
        hmid = _mm_fullk(xn2, p["w_up"], l, epilogue="relu2_bf16")
        f = _mm_ktiled(hmid, p["w_down"], l)
        if l + 1 == n_a:
            x, (xkv, xn) = _resid_norm(x, f, p["norm_mlp_post"][l], [p["norm_kv"], p["norm_mix_pre"][l + 1]])
            kv = _shared_kv(xkv, B, T, H, hd, past, p)
        elif l + 1 < depth:
            x, (xn,) = _resid_norm(x, f, p["norm_mlp_post"][l], [p["norm_mix_pre"][l + 1]])
        else:
            x, _ = _resid_norm(x, f, p["norm_mlp_post"][l], [])
    return (x.reshape(B, T, D), jnp.stack(hs), jnp.stack(convs),
            kv["k"].reshape(B, T, H, hd), kv["v"].reshape(B, T, H, hd), kv["logf"])


def _shared_kv(xkv, B, T, H, hd, past, p):
    d_attn = H * hd
    L = V7X_LANES
    w_kvf = p["w_kvf"]
    tn = min(512, d_attn)
    k, kb = _mm_fullk(xkv, w_kvf, n_cols=d_attn, col_block0=0, epilogue="f32_bf16")
    v, vb = _mm_fullk(xkv, w_kvf, n_cols=d_attn, col_block0=d_attn // tn, epilogue="f32_bf16")
    w_f = jnp.pad(w_kvf[:, 2 * d_attn:], ((0, 0), (0, L - H)))
    b_f = jnp.pad(p["b_f"], (0, L - H)).reshape(1, L)
    f = _mm_fullk(xkv, w_f, epilogue="f32")
    kv = dict(k=k, v=v, kb=kb)
    if past is None:
        logf, cum_col, cum_row = _logf_cum(f.reshape(B, T, L), b_f)
        vt = jnp.swapaxes(vb.reshape(B, T, d_attn), 1, 2)
        kv.update(logf=logf[:, :, :H], cum_col=cum_col, cum_row=cum_row, vt=vt)
    else:
        cache_k, cache_v, cache_logf, page_table = past
        cum_past, tot = _paged_cum(cache_logf, page_table)
        logf, cum_new = _logf_new(f, b_f, tot)
        kv.update(logf=logf[:, :H].reshape(B, T, H), cum_past=cum_past, cum_new=cum_new,
                  cache_k=cache_k, cache_v=cache_v, page_table=page_table)
    return kv


def _logf_new_kernel(f_ref, bf_ref, tot_ref, lf_ref, cn_ref):
    lp = _log_sigmoid(f_ref[...] + bf_ref[...])
    lf_ref[...] = lp
    cn_ref[...] = tot_ref[...] + lp[:, :tot_ref.shape[1]]


def _logf_new(f, b_f_pad, tot):
    DB, L = f.shape
    H = tot.shape[1]
    return pl.pallas_call(
        _logf_new_kernel,
        out_shape=[jax.ShapeDtypeStruct((DB, L), F32), jax.ShapeDtypeStruct((DB, H), F32)],
        compiler_params=pltpu.CompilerParams(vmem_limit_bytes=VMEM_LIMIT),
        name="logf_new",
    )(f, b_f_pad, tot)


def kernel(x_prompt, x_sample, state_rglru_h, state_conv, cache_k, cache_v, cache_logf, page_table,
           w_in, conv_w, conv_b, gate_w, gate_b, rg_lambda, w_out_a, norm_kv, w_kvf, b_f, w_q, w_o,
           norm_mix_pre, norm_mix_post, norm_mlp_pre, norm_mlp_post, w_up, w_down):
    p = dict(w_in=w_in, conv_w=conv_w, conv_b=conv_b, gate_w=gate_w, gate_b=gate_b, rg_lambda=rg_lambda,
             w_out_a=w_out_a, norm_kv=norm_kv, w_kvf=w_kvf, b_f=b_f, w_q=w_q, w_o=w_o,
             norm_mix_pre=norm_mix_pre, norm_mix_post=norm_mix_post, norm_mlp_pre=norm_mlp_pre,
             norm_mlp_post=norm_mlp_post, w_up=w_up, w_down=w_down)
    n_a, _, d_rnn = state_rglru_h.shape
    nw = conv_w.shape[1]
    B = x_prompt.shape[0]
    h0 = jnp.zeros((n_a, B, d_rnn), F32)
    c0 = jnp.zeros((n_a, B, nw - 1, d_rnn), F32)
    out_p = _run_group(x_prompt, h0, c0, None, p)
    out_s = _run_group(x_sample, state_rglru_h, state_conv, (cache_k, cache_v, cache_logf, page_table), p)
    return (out_p[0], out_s[0]) + out_p[1:] + out_s[1:]
```

```python
import functools
import math

import jax
import jax.numpy as jnp
from jax import lax
from jax.experimental import pallas as pl
from jax.experimental.pallas import tpu as pltpu

F32 = jnp.float32
BF16 = jnp.bfloat16

RMS_EPS = 1e-6
LRU_C = 8.0

V7X_VMEM_BYTES = 64 * 1024 * 1024
V7X_LANES = 128
V7X_SUBLANES = 8
VMEM_LIMIT = 56 * 1024 * 1024


def _params(*sem):
    return pltpu.CompilerParams(dimension_semantics=sem, vmem_limit_bytes=VMEM_LIMIT)


def _rms_scale(x):
    return x * lax.rsqrt(jnp.mean(x * x, axis=-1, keepdims=True) + RMS_EPS)


def _log_sigmoid(x):
    return jnp.minimum(x, 0.0) - jnp.log1p(jnp.exp(-jnp.abs(x)))


def _gelu_tanh(x):
    c = math.sqrt(2.0 / math.pi)
    return 0.5 * x * (1.0 + jnp.tanh(c * (x + 0.044715 * (x * x * x))))


def _norm_cast_kernel(x_ref, g_ref, o_ref):
    o_ref[...] = (_rms_scale(x_ref[...]) * g_ref[...]).astype(o_ref.dtype)


def _norm_cast(x, g):
    M, D = x.shape
    tr = min(256, M)
    return pl.pallas_call(
        _norm_cast_kernel,
        grid=(M // tr,),
        in_specs=[pl.BlockSpec((tr, D), lambda i: (i, 0)), pl.BlockSpec((1, D), lambda i: (0, 0))],
        out_specs=pl.BlockSpec((tr, D), lambda i: (i, 0)),
        out_shape=jax.ShapeDtypeStruct((M, D), BF16),
        compiler_params=_params("parallel"),
        name="norm_cast",
    )(x, g.reshape(1, D))


def _resid_norm_kernel(x_ref, m_ref, gp_ref, gn_ref, xo_ref, *n_refs):
    xn = x_ref[...] + _rms_scale(m_ref[...]) * gp_ref[...]
    xo_ref[...] = xn
    if n_refs:
        y = _rms_scale(xn)
        for j, r in enumerate(n_refs):
            r[...] = (y * gn_ref[j:j + 1, :]).astype(r.dtype)


def _resid_norm(x, m, g_post, g_next):
    M, D = x.shape
    n = len(g_next)
    tr = min(256, M)
    gn = jnp.stack(g_next) if n else jnp.zeros((1, D), F32)
    row = pl.BlockSpec((tr, D), lambda i: (i, 0))
    outs = pl.pallas_call(
        _resid_norm_kernel,
        grid=(M // tr,),
        in_specs=[row, row, pl.BlockSpec((1, D), lambda i: (0, 0)),
                  pl.BlockSpec((gn.shape[0], D), lambda i: (0, 0))],
        out_specs=[row] * (1 + n),
        out_shape=[jax.ShapeDtypeStruct((M, D), F32)] + [jax.ShapeDtypeStruct((M, D), BF16)] * n,
        compiler_params=_params("parallel"),
        name="resid_norm",
    )(x, m, g_post.reshape(1, D), gn)
    return outs[0], list(outs[1:])


def _mm_store(acc, o_refs, epilogue):
    if epilogue == "f32":
        o_refs[0][...] = acc
    elif epilogue == "bf16":
        o_refs[0][...] = acc.astype(BF16)
    elif epilogue == "f32_bf16":
        o_refs[0][...] = acc
        o_refs[1][...] = acc.astype(BF16)
    elif epilogue == "f32_bf16t":
        o_refs[0][...] = acc
        o_refs[1][...] = acc.T.astype(BF16)
    elif epilogue == "relu2_bf16":
        h = jnp.maximum(acc, 0.0)
        o_refs[0][...] = (h * h).astype(BF16)
    else:
        raise ValueError(epilogue)


_EPILOGUE_DTYPES = {"f32": (F32,), "bf16": (BF16,), "f32_bf16": (F32, BF16), "f32_bf16t": (F32, BF16),
                    "relu2_bf16": (BF16,)}


def _mm_fullk_kernel(*refs, epilogue, epilogue2):
    n1 = len(_EPILOGUE_DTYPES[epilogue])
    if epilogue2 is None:
        a_ref, w_ref, *o_refs, wb_ref = refs
    else:
        a_ref, a2_ref, w_ref, *o_refs, wb_ref = refs

    @pl.when(pl.program_id(1) == 0)
    def _():
        wb_ref[...] = w_ref[...].astype(BF16)
        if epilogue2 is not None:
            acc2 = jnp.dot(a2_ref[...], wb_ref[...], preferred_element_type=F32)
            _mm_store(acc2, o_refs[n1:], epilogue2)

    acc = jnp.dot(a_ref[...], wb_ref[...], preferred_element_type=F32)
    _mm_store(acc, o_refs[:n1], epilogue)


def _mm_fullk(a, w, layer=0, *, a2=None, n_cols=None, col_block0=0, epilogue="f32", epilogue2=None):
    if w.ndim == 2:
        w = w[None]
    M, K = a.shape
    N = w.shape[2] if n_cols is None else n_cols
    tm = min(1024, M)
    tn = min(512, N)
    assert M % tm == 0 and N % tn == 0
    dts = _EPILOGUE_DTYPES[epilogue]
    in_specs = [pl.BlockSpec((tm, K), lambda n, m: (m, 0))]
    out_specs = [pl.BlockSpec((tm, tn), lambda n, m: (m, n))] * len(dts)
    out_shape = [jax.ShapeDtypeStruct((M, N), d) for d in dts]
    if epilogue == "f32_bf16t":
        out_specs[1] = pl.BlockSpec((tn, tm), lambda n, m: (n, m))
        out_shape[1] = jax.ShapeDtypeStruct((N, M), dts[1])
    args = [a]
    if a2 is not None:
        epilogue2 = epilogue2 or epilogue
        M2 = a2.shape[0]
        dts2 = _EPILOGUE_DTYPES[epilogue2]
        in_specs.append(pl.BlockSpec((M2, K), lambda n, m: (0, 0)))
        out_specs += [pl.BlockSpec((M2, tn), lambda n, m: (0, n))] * len(dts2)
        out_shape += [jax.ShapeDtypeStruct((M2, N), d) for d in dts2]
        args.append(a2)
    in_specs.append(pl.BlockSpec((None, K, tn), lambda n, m: (layer, 0, n + col_block0)))
    outs = pl.pallas_call(
        functools.partial(_mm_fullk_kernel, epilogue=epilogue, epilogue2=epilogue2 if a2 is not None else None),
        grid=(N // tn, M // tm),
        in_specs=in_specs,
        out_specs=out_specs,
        out_shape=out_shape,
        scratch_shapes=[pltpu.VMEM((K, tn), BF16)],
        compiler_params=_params("arbitrary", "arbitrary"),
        name="mm_fullk_" + epilogue,
    )(*args, w)
    unpack = lambda o: o if len(o) > 1 else o[0]
    if a2 is None:
        return unpack(outs)
    return unpack(outs[:len(dts)]), unpack(outs[len(dts):])


def _mm_ktiled_kernel(*refs, with2):
    if with2:
        a_ref, a2_ref, w_ref, o_ref, o2_ref = refs
    else:
        a_ref, w_ref, o_ref = refs
    first_k = pl.program_id(2) == 0
    wb = w_ref[...].astype(BF16)
    part = jnp.dot(a_ref[...], wb, preferred_element_type=F32)

    @pl.when(first_k)
    def _():
        o_ref[...] = part

    @pl.when(jnp.logical_not(first_k))
    def _():
        o_ref[...] += part

    if with2:
        first_m = pl.program_id(0) == 0

        @pl.when(first_m & first_k)
        def _():
            o2_ref[...] = jnp.dot(a2_ref[...], wb, preferred_element_type=F32)

        @pl.when(first_m & jnp.logical_not(first_k))
        def _():
            o2_ref[...] += jnp.dot(a2_ref[...], wb, preferred_element_type=F32)


def _mm_ktiled(a, w, layer, a2=None):
    M, K = a.shape
    N = w.shape[2]
    tm = min(1024, M)
    tn = min(1024, N)
    tk = min(2048, K)
    assert M % tm == 0 and N % tn == 0 and K % tk == 0
    in_specs = [pl.BlockSpec((tm, tk), lambda m, n, k: (m, k))]
    out_specs = [pl.BlockSpec((tm, tn), lambda m, n, k: (m, n))]
    out_shape = [jax.ShapeDtypeStruct((M, N), F32)]
    args = [a]
    if a2 is not None:
        M2 = a2.shape[0]
        last_n, last_k = N // tn - 1, K // tk - 1
        in_specs.append(pl.BlockSpec((M2, tk), lambda m, n, k: (0, jnp.where(m == 0, k, last_k))))
        out_specs.append(pl.BlockSpec((M2, tn), lambda m, n, k: (0, jnp.where(m == 0, n, last_n))))
        out_shape.append(jax.ShapeDtypeStruct((M2, N), F32))
        args.append(a2)
    in_specs.append(pl.BlockSpec((None, tk, tn), lambda m, n, k: (layer, k, n)))
    outs = pl.pallas_call(
        functools.partial(_mm_ktiled_kernel, with2=a2 is not None),
        grid=(M // tm, N // tn, K // tk),
        in_specs=in_specs,
        out_specs=out_specs,
        out_shape=out_shape,
        compiler_params=_params("arbitrary", "arbitrary", "arbitrary"),
        name="mm_ktiled",
    )(*args, w)
    return outs[0] if a2 is None else (outs[0], outs[1])


def _rglru_coeffs(conv, gw_ref, gb_ref, lam_ref):
    cb = conv.astype(BF16)
    gi = jnp.dot(cb, gw_ref[0, 0].astype(BF16), preferred_element_type=F32) + gb_ref[0:1, :]
    gr = jnp.dot(cb, gw_ref[1, 0].astype(BF16), preferred_element_type=F32) + gb_ref[1:2, :]
    i_gate = jax.nn.sigmoid(gi)
    r_gate = jax.nn.sigmoid(gr)
    log_a = LRU_C * r_gate * _log_sigmoid(lam_ref[...])
    a = jnp.exp(log_a)
    th = jnp.tanh(log_a)
    mult = jnp.sqrt(-2.0 * th / (1.0 - th))
    return a, mult * i_gate * conv


def _rglru_seq_kernel(x_ref, y_ref, h0_ref, c0_ref, cw_ref, cb_ref, gw_ref, gb_ref, lam_ref,
                      hg_ref, hl_ref, xs_ref, hc_ref, *, tT):
    t = pl.program_id(2)
    S = V7X_SUBLANES

    @pl.when(t == 0)
    def _():
        xs_ref[0:S, :] = c0_ref[0]
        hc_ref[...] = h0_ref[0]

    x = x_ref[0]
    xs_ref[S:S + tT, :] = x
    cw = cw_ref[...]
    nw = cw.shape[0]
    conv = cb_ref[...] + cw[nw - 1:nw, :] * x
    for j in range(1, nw):
        conv = conv + cw[nw - 1 - j:nw - j, :] * xs_ref[S - j:S - j + tT, :]
    xs_ref[0:S, :] = xs_ref[tT:tT + S, :]

    a, b = _rglru_coeffs(conv, gw_ref, gb_ref, lam_ref)

    row = lax.broadcasted_iota(jnp.int32, (S, a.shape[1]), 0)
    h = hc_ref[...]
    hs = []
    for r in range(0, tT, S):
        a8 = a[r:r + S]
        b8 = b[r:r + S]
        s = 1
        while s < S:
            keep = row >= s
            b8 = jnp.where(keep, a8 * pltpu.roll(b8, s, 0) + b8, b8)
            a8 = jnp.where(keep, a8 * pltpu.roll(a8, s, 0), a8)
            s *= 2
        h8 = a8 * h + b8
        hs.append(h8)
        h = h8[S - 1:S, :]
    hc_ref[...] = h
    hl_ref[0] = h
    hg_ref[0] = (jnp.concatenate(hs, axis=0) * _gelu_tanh(y_ref[0])).astype(hg_ref.dtype)


def _rglru_seq(xy, h0, conv0, conv_w, conv_b, gate_w, gate_b, lam):
    B, T, C2 = xy.shape
    C = C2 // 2
    nblk, cblk = gate_w.shape[1], gate_w.shape[2]
    nw = conv_w.shape[0]
    S = V7X_SUBLANES
    tT = min(512, T)
    assert T % tT == 0 and tT % S == 0 and nw - 1 <= S and cblk * nblk == C
    c0 = jnp.concatenate([jnp.zeros((B, S - (nw - 1), C), F32), conv0], axis=1)
    hg, hl = pl.pallas_call(
        functools.partial(_rglru_seq_kernel, tT=tT),
        grid=(B, nblk, T // tT),
        in_specs=[
            pl.BlockSpec((1, tT, cblk), lambda b, n, t: (b, t, n)),
            pl.BlockSpec((1, tT, cblk), lambda b, n, t: (b, t, n + nblk)),
            pl.BlockSpec((1, 1, cblk), lambda b, n, t: (b, 0, n)),
            pl.BlockSpec((1, S, cblk), lambda b, n, t: (b, 0, n)),
            pl.BlockSpec((nw, cblk), lambda b, n, t: (0, n)),
            pl.BlockSpec((1, cblk), lambda b, n, t: (0, n)),
            pl.BlockSpec((2, 1, cblk, cblk), lambda b, n, t: (0, n, 0, 0)),
            pl.BlockSpec((2, cblk), lambda b, n, t: (0, n)),
            pl.BlockSpec((1, cblk), lambda b, n, t: (0, n)),
        ],
        out_specs=[pl.BlockSpec((1, tT, cblk), lambda b, n, t: (b, t, n)),
                   pl.BlockSpec((1, 1, cblk), lambda b, n, t: (b, 0, n))],
        out_shape=[jax.ShapeDtypeStruct((B, T, C), BF16), jax.ShapeDtypeStruct((B, 1, C), F32)],
        scratch_shapes=[pltpu.VMEM((tT + S, cblk), F32), pltpu.VMEM((1, cblk), F32)],
        compiler_params=_params("parallel", "parallel", "arbitrary"),
        name="rglru_seq",
    )(xy, xy, h0.reshape(B, 1, C), c0, conv_w, conv_b.reshape(1, C), gate_w, gate_b, lam.reshape(1, C))
    return hg, hl.reshape(B, C)


def _rglru_step_kernel(x_ref, y_ref, h0_ref, c0_ref, cw_ref, cb_ref, gw_ref, gb_ref, lam_ref,
                       hg_ref, hl_ref):
    x = x_ref[...]
    cw = cw_ref[...]
    nw = cw.shape[0]
    conv = cb_ref[...] + cw[nw - 1:nw, :] * x
    for j in range(1, nw):
        conv = conv + cw[nw - 1 - j:nw - j, :] * c0_ref[nw - 1 - j]
    a, b = _rglru_coeffs(conv, gw_ref, gb_ref, lam_ref)
    h = a * h0_ref[...] + b
    hl_ref[...] = h
    hg_ref[...] = (h * _gelu_tanh(y_ref[...])).astype(hg_ref.dtype)


def _rglru_step(xy, h0, conv0, conv_w, conv_b, gate_w, gate_b, lam):
    B, C2 = xy.shape
    C = C2 // 2
    nblk, cblk = gate_w.shape[1], gate_w.shape[2]
    nw = conv_w.shape[0]
    c0 = jnp.swapaxes(conv0, 0, 1)
    return pl.pallas_call(
        _rglru_step_kernel,
        grid=(nblk,),
        in_specs=[
            pl.BlockSpec((B, cblk), lambda n: (0, n)),
            pl.BlockSpec((B, cblk), lambda n: (0, n + nblk)),
            pl.BlockSpec((B, cblk), lambda n: (0, n)),
            pl.BlockSpec((nw - 1, B, cblk), lambda n: (0, 0, n)),
            pl.BlockSpec((nw, cblk), lambda n: (0, n)),
            pl.BlockSpec((1, cblk), lambda n: (0, n)),
            pl.BlockSpec((2, 1, cblk, cblk), lambda n: (0, n, 0, 0)),
            pl.BlockSpec((2, cblk), lambda n: (0, n)),
            pl.BlockSpec((1, cblk), lambda n: (0, n)),
        ],
        out_specs=[pl.BlockSpec((B, cblk), lambda n: (0, n)), pl.BlockSpec((B, cblk), lambda n: (0, n))],
        out_shape=[jax.ShapeDtypeStruct((B, C), BF16), jax.ShapeDtypeStruct((B, C), F32)],
        compiler_params=_params("parallel"),
        name="rglru_step",
    )(xy, xy, h0, c0, conv_w, conv_b.reshape(1, C), gate_w, gate_b, lam.reshape(1, C))


def _tri_dots(lp, carry_col_ref, carry_row_ref, cum_col_ref, cum_row_ref):
    P = lp.shape[0]
    hi = lax.Precision.HIGHEST
    r = lax.broadcasted_iota(jnp.int32, (P, P), 0)
    c = lax.broadcasted_iota(jnp.int32, (P, P), 1)
    ones = jnp.ones((P, P), F32)
    if cum_col_ref is not None:
        tril = (c <= r).astype(F32)
        cum_col_ref[0] = jnp.dot(tril, lp, precision=hi, preferred_element_type=F32) + carry_col_ref[...]
        carry_col_ref[...] += jnp.dot(ones, lp, precision=hi, preferred_element_type=F32)
    lpt = lp.T
    triu = (r <= c).astype(F32)
    cum_row_ref[0] = jnp.dot(lpt, triu, precision=hi, preferred_element_type=F32) + carry_row_ref[...]
    carry_row_ref[...] += jnp.dot(lpt, ones, precision=hi, preferred_element_type=F32)


def _logf_cum_kernel(f_ref, bf_ref, lf_ref, cc_ref, cr_ref, car_c, car_r):
    @pl.when(pl.program_id(1) == 0)
    def _():
        car_c[...] = jnp.zeros_like(car_c)
        car_r[...] = jnp.zeros_like(car_r)

    lp = _log_sigmoid(f_ref[0] + bf_ref[...])
    lf_ref[0] = lp
    _tri_dots(lp, car_c, car_r, cc_ref, cr_ref)


def _logf_cum(f, b_f_pad):
    B, T, L = f.shape
    P = V7X_LANES
    assert T % P == 0 and L == P
    return pl.pallas_call(
        _logf_cum_kernel,
        grid=(B, T // P),
        in_specs=[pl.BlockSpec((1, P, L), lambda b, t: (b, t, 0)), pl.BlockSpec((1, L), lambda b, t: (0, 0))],
        out_specs=[pl.BlockSpec((1, P, L), lambda b, t: (b, t, 0)),
                   pl.BlockSpec((1, P, L), lambda b, t: (b, t, 0)),
                   pl.BlockSpec((1, L, P), lambda b, t: (b, 0, t))],
        out_shape=[jax.ShapeDtypeStruct((B, T, L), F32), jax.ShapeDtypeStruct((B, T, L), F32),
                   jax.ShapeDtypeStruct((B, L, T), F32)],
        scratch_shapes=[pltpu.VMEM((P, L), F32), pltpu.VMEM((L, P), F32)],
        compiler_params=_params("parallel", "arbitrary"),
        name="logf_cum",
    )(f, b_f_pad)


def _paged_cum_kernel(pt_ref, *refs, n_in):
    lf_refs = refs[:n_in]
    cc_ref, tot_ref, car = refs[n_in:]

    @pl.when(pl.program_id(1) == 0)
    def _():
        car[...] = jnp.zeros_like(car)

    P = lf_refs[0].shape[1]
    hi = lax.Precision.HIGHEST
    tril = (lax.broadcasted_iota(jnp.int32, (P, P), 1) <= lax.broadcasted_iota(jnp.int32, (P, P), 0)).astype(F32)
    ones = jnp.ones((P, P), F32)
    for g, lf_ref in enumerate(lf_refs):
        lp = lf_ref[0]
        cc_ref[0, g * P:(g + 1) * P, :] = jnp.dot(tril, lp, precision=hi, preferred_element_type=F32) + car[...]
        car[...] += jnp.dot(ones, lp, precision=hi, preferred_element_type=F32)
    tot_ref[0] = car[0:V7X_SUBLANES, :]


def _paged_cum(cache_logf, page_table):
    n_pool, P, H = cache_logf.shape
    DB, n_pages = page_table.shape
    G = math.gcd(n_pages, 8)
    S = V7X_SUBLANES
    specs = [pl.BlockSpec((1, P, H), lambda b, s, pt, g=g: (pt[b, s * G + g], 0, 0)) for g in range(G)]
    cum, tot = pl.pallas_call(
        functools.partial(_paged_cum_kernel, n_in=G),
        grid_spec=pltpu.PrefetchScalarGridSpec(
            num_scalar_prefetch=1,
            grid=(DB, n_pages // G),
            in_specs=specs,
            out_specs=[pl.BlockSpec((1, G * P, H), lambda b, s, pt: (b, s, 0)),
                       pl.BlockSpec((1, S, H), lambda b, s, pt: (b, 0, 0))],
            scratch_shapes=[pltpu.VMEM((P, H), F32)],
        ),
        out_shape=[jax.ShapeDtypeStruct((DB, n_pages * P, H), F32), jax.ShapeDtypeStruct((DB, S, H), F32)],
        compiler_params=_params("parallel", "arbitrary"),
        name="paged_cum",
    )(page_table, *([cache_logf] * G))
    return cum, tot[:, 0, :]


def _fox_prompt_kernel(q_ref, k_ref, vt_ref, cq_ref, ck_ref, o_ref, *, tq, hd, hb, nq, scale):
    qi = pl.program_id(2)
    nt = (((1,), (1,)), ((), ()))
    below_or_on = (lax.broadcasted_iota(jnp.int32, (tq, 1), 0)
                   <= lax.broadcasted_iota(jnp.int32, (1, tq), 1))

    def q_block(c):
        lo = c * tq
        for j in range(hb):
            cols = slice(j * hd, (j + 1) * hd)
            qj = q_ref[0, :, cols]
            cqj = cq_ref[0, 0, j:j + 1, :]
            sd = lax.dot_general(k_ref[0, lo:lo + tq, cols], qj, nt, preferred_element_type=F32) * scale
            sd = sd + (cqj - ck_ref[0, 0, lo:lo + tq, j:j + 1])
            sd = jnp.where(below_or_on, sd, -jnp.inf)
            m = jnp.max(sd, axis=0, keepdims=True)
            if c:
                sp = lax.dot_general(k_ref[0, 0:lo, cols], qj, nt, preferred_element_type=F32) * scale
                sp = sp + (cqj - ck_ref[0, 0, 0:lo, j:j + 1])
                m = jnp.maximum(m, jnp.max(sp, axis=0, keepdims=True))
            pd = jnp.exp(sd - m)
            l = jnp.sum(pd, axis=0, keepdims=True)
            acc = jnp.dot(vt_ref[cols, lo:lo + tq], pd.astype(BF16), preferred_element_type=F32)
            if c:
                pp = jnp.exp(sp - m)
                l = l + jnp.sum(pp, axis=0, keepdims=True)
                acc = acc + jnp.dot(vt_ref[cols, 0:lo], pp.astype(BF16), preferred_element_type=F32)
            o_ref[0, :, cols] = (acc / l).T.astype(o_ref.dtype)

    for c in range(nq):
        pl.when(qi == c)(functools.partial(q_block, c))


def _fox_prompt(q, k, vt, cum_col, cum_row, n_heads):
    B, T, D = q.shape
    hd = D // n_heads
    hb = min(4, n_heads)
    hg = n_heads // hb
    tq = min(256, T)
    assert n_heads % hb == 0 and T % tq == 0
    cq = cum_row[:, :n_heads, :].reshape(B, hg, hb, T)
    ck = jnp.transpose(cum_col[:, :, :n_heads].reshape(B, T, hg, hb), (0, 2, 1, 3))
    return pl.pallas_call(
        functools.partial(_fox_prompt_kernel, tq=tq, hd=hd, hb=hb, nq=T // tq, scale=hd ** -0.5),
        grid=(B, hg, T // tq),
        in_specs=[pl.BlockSpec((1, tq, hb * hd), lambda b, g, i: (b, i, g)),
                  pl.BlockSpec((1, T, hb * hd), lambda b, g, i: (b, 0, g)),
                  pl.BlockSpec((hb * hd, T), lambda b, g, i: (g, b)),
                  pl.BlockSpec((1, 1, hb, tq), lambda b, g, i: (b, g, 0, i)),
                  pl.BlockSpec((1, 1, T, hb), lambda b, g, i: (b, g, 0, 0))],
        out_specs=pl.BlockSpec((1, tq, hb * hd), lambda b, g, i: (b, i, g)),
        out_shape=jax.ShapeDtypeStruct((B, T, D), BF16),
        compiler_params=_params("parallel", "parallel", "arbitrary"),
        name="fox_prompt",
    )(q, k, vt, cq, ck)


def _fox_decode_kernel(pt_ref, q_ref, k_ref, v_ref, ckf_ref, cqf_ref, cqb_ref, kn_ref, vn_ref,
                       o_ref, m_ref, l_ref, acc_ref, *, scale):
    p_idx = pl.program_id(1)
    H = q_ref.shape[1]

    @pl.when(p_idx == 0)
    def _():
        m_ref[...] = jnp.full_like(m_ref, -jnp.inf)
        l_ref[...] = jnp.zeros_like(l_ref)
        acc_ref[...] = jnp.zeros_like(acc_ref)

    qb = q_ref[0].astype(BF16)
    kb = k_ref[0].astype(BF16)
    s = lax.dot_general(qb, kb, (((1,), (1,)), ((), ())), preferred_element_type=F32) * scale
    s = s + (cqf_ref[0] - ckf_ref[0, 0])
    own = (lax.broadcasted_iota(jnp.int32, s.shape, 1) % H) == lax.broadcasted_iota(jnp.int32, s.shape, 0)
    s = jnp.where(own, s, -jnp.inf)
    m_prev = m_ref[...]
    m_new = jnp.maximum(m_prev, jnp.max(s, axis=1, keepdims=True))
    alpha = jnp.exp(m_prev - m_new)
    p = jnp.exp(s - m_new[:, 0:1])
    l_ref[...] = alpha * l_ref[...] + jnp.sum(p, axis=1, keepdims=True)
    m_ref[...] = m_new
    acc_ref[...] = alpha * acc_ref[...] + jnp.dot(p.astype(BF16), v_ref[0].astype(BF16),
                                                  preferred_element_type=F32)

    @pl.when(p_idx == pl.num_programs(1) - 1)
    def _():
        qn = q_ref[0].astype(BF16).astype(F32)
        kn = kn_ref[0].astype(BF16).astype(F32)
        vn = vn_ref[0].astype(BF16).astype(F32)
        cq = cqb_ref[0]
        s_n = jnp.sum(qn * kn, axis=1, keepdims=True) * scale + (cq - cq)
        m_p = m_ref[...]
        m_f = jnp.maximum(m_p, s_n)
        al = jnp.exp(m_p - m_f)
        p_n = jnp.exp(s_n - m_f)
        l_f = al * l_ref[...] + p_n
        o = al * acc_ref[...] + p_n.astype(BF16).astype(F32) * vn
        o_ref[0] = (o / l_f).astype(o_ref.dtype)


def _fox_decode(q, k_new, v_new, cache_k, cache_v, page_table, cum_past, cum_new):
    DB, H, hd = q.shape
    n_pool, P = cache_k.shape[0], cache_k.shape[1]
    n_pages = page_table.shape[1]
    R = P * H
    assert hd == V7X_LANES
    k2 = cache_k.reshape(n_pool, R, hd)
    v2 = cache_v.reshape(n_pool, R, hd)
    ckf = cum_past.reshape(DB, n_pages, 1, R)
    cqf = jnp.tile(cum_new, (1, P)).reshape(DB, 1, R)
    cqb = jnp.broadcast_to(cum_new[:, :, None], (DB, H, hd))
    per_seq = lambda shape: pl.BlockSpec((1,) + shape, lambda b, p, pt: (b, 0, 0))
    return pl.pallas_call(
        functools.partial(_fox_decode_kernel, scale=hd ** -0.5),
        grid_spec=pltpu.PrefetchScalarGridSpec(
            num_scalar_prefetch=1,
            grid=(DB, n_pages),
            in_specs=[per_seq((H, hd)),
                      pl.BlockSpec((1, R, hd), lambda b, p, pt: (pt[b, p], 0, 0)),
                      pl.BlockSpec((1, R, hd), lambda b, p, pt: (pt[b, p], 0, 0)),
                      pl.BlockSpec((1, 1, 1, R), lambda b, p, pt: (b, p, 0, 0)),
                      per_seq((1, R)), per_seq((H, hd)), per_seq((H, hd)), per_seq((H, hd))],
            out_specs=per_seq((H, hd)),
            scratch_shapes=[pltpu.VMEM((H, hd), F32), pltpu.VMEM((H, hd), F32), pltpu.VMEM((H, hd), F32)],
        ),
        out_shape=jax.ShapeDtypeStruct((DB, H, hd), BF16),
        compiler_params=_params("parallel", "arbitrary"),
        name="fox_decode",
    )(page_table, q, k2, v2, ckf, cqf, cqb, k_new, v_new)


def _forward(x_prompt, x_sample, h0_s, conv0_s, cache_k, cache_v, cache_logf, page_table, p):
    B, T, D = x_prompt.shape
    DB, DT, _ = x_sample.shape
    assert DT == 1, "the sample group carries one new token per sequence"
    M = B * T
    depth = p["w_up"].shape[0]
    n_a = p["w_in"].shape[0]
    H = p["b_f"].shape[0]
    hd = p["w_q"].shape[2] // H
    d_attn = H * hd
    nw = p["conv_w"].shape[1]
    assert T >= nw - 1

    xp = x_prompt.reshape(M, D)
    xs = x_sample.reshape(DB, D)
    xnp = _norm_cast(xp, p["norm_mix_pre"][0])
    xns = _norm_cast(xs, p["norm_mix_pre"][0])
    hs_p, convs_p, hs_s, convs_s = [], [], [], []
    kv = None
    for l in range(depth):
        if l < n_a:
            xy_p, xy_s = _mm_fullk(xnp, p["w_in"], l, a2=xns, epilogue="f32")
            C = xy_p.shape[1] // 2
            args = (p["conv_w"][l], p["conv_b"][l], p["gate_w"][l], p["gate_b"][l], p["rg_lambda"][l])
            xy_p3 = xy_p.reshape(B, T, 2 * C)
            hg_p, h_p = _rglru_seq(xy_p3, jnp.zeros((B, C), F32), jnp.zeros((B, nw - 1, C), F32), *args)
            hg_s, h_s = _rglru_step(xy_s, h0_s[l], conv0_s[l], *args)
            hs_p.append(h_p)
            hs_s.append(h_s)
            convs_p.append(xy_p3[:, T - (nw - 1):, :C])
            convs_s.append(jnp.concatenate([conv0_s[l], xy_s[:, None, :C]], axis=1)[:, 1:])
            m_p, m_s = _mm_fullk(hg_p.reshape(M, C), p["w_out_a"], l, a2=hg_s, epilogue="f32")
        else:
            j = l - n_a
            q_p, q_s = _mm_fullk(xnp, p["w_q"], j, a2=xns, epilogue="bf16", epilogue2="f32")
            o_p = _fox_prompt(q_p.reshape(B, T, d_attn), kv["kb"].reshape(B, T, d_attn), kv["vt"],
                              kv["cum_col"], kv["cum_row"], H)
            o_s = _fox_decode(q_s.reshape(DB, H, hd), kv["k_s"].reshape(DB, H, hd), kv["v_s"].reshape(DB, H, hd),
                              cache_k, cache_v, page_table, kv["cum_past"], kv["cum_new"])
            m_p, m_s = _mm_fullk(o_p.reshape(M, d_attn), p["w_o"], j, a2=o_s.reshape(DB, d_attn), epilogue="f32")
        xp, (xn2p,) = _resid_norm(xp, m_p, p["norm_mix_post"][l], [p["norm_mlp_pre"][l]])
        xs, (xn2s,) = _resid_norm(xs, m_s, p["norm_mix_post"][l], [p["norm_mlp_pre"][l]])
        hmid_p, hmid_s = _mm_fullk(xn2p, p["w_up"], l, a2=xn2s, epilogue="relu2_bf16")
        f_p, f_s = _mm_ktiled(hmid_p, p["w_down"], l, a2=hmid_s)
        if l + 1 == n_a:
            gains = [p["norm_kv"], p["norm_mix_pre"][l + 1]]
            xp, (xkv_p, xnp) = _resid_norm(xp, f_p, p["norm_mlp_post"][l], gains)
            xs, (xkv_s, xns) = _resid_norm(xs, f_s, p["norm_mlp_post"][l], gains)
            kv = _shared_kv(xkv_p, xkv_s, B, T, H, hd, cache_logf, page_table, p)
        elif l + 1 < depth:
            xp, (xnp,) = _resid_norm(xp, f_p, p["norm_mlp_post"][l], [p["norm_mix_pre"][l + 1]])
            xs, (xns,) = _resid_norm(xs, f_s, p["norm_mlp_post"][l], [p["norm_mix_pre"][l + 1]])
        else:
            xp, _ = _resid_norm(xp, f_p, p["norm_mlp_post"][l], [])
            xs, _ = _resid_norm(xs, f_s, p["norm_mlp_post"][l], [])
    return (xp.reshape(B, T, D), xs.reshape(DB, DT, D),
            jnp.stack(hs_p), jnp.stack(convs_p),
            kv["k_p"].reshape(B, T, H, hd), kv["v_p"].reshape(B, T, H, hd), kv["logf_p"],
            jnp.stack(hs_s), jnp.stack(convs_s),
            kv["k_s"].reshape(DB, DT, H, hd), kv["v_s"].reshape(DB, DT, H, hd), kv["logf_s"])


def _shared_kv(xkv_p, xkv_s, B, T, H, hd, cache_logf, page_table, p):
    d_attn = H * hd
    DB = xkv_s.shape[0]
    L = V7X_LANES
    w_kvf = p["w_kvf"]
    tn = min(512, d_attn)
    (k_p, kb), k_s = _mm_fullk(xkv_p, w_kvf, a2=xkv_s, n_cols=d_attn, col_block0=0,
                               epilogue="f32_bf16", epilogue2="f32")
    (v_p, vt), v_s = _mm_fullk(xkv_p, w_kvf, a2=xkv_s, n_cols=d_attn, col_block0=d_attn // tn,
                               epilogue="f32_bf16t", epilogue2="f32")
    w_f = jnp.pad(w_kvf[:, 2 * d_attn:], ((0, 0), (0, L - H)))
    b_f = jnp.pad(p["b_f"], (0, L - H)).reshape(1, L)
    f_p, f_s = _mm_fullk(xkv_p, w_f, a2=xkv_s, epilogue="f32")
    logf_p, cum_col, cum_row = _logf_cum(f_p.reshape(B, T, L), b_f)
    cum_past, tot = _paged_cum(cache_logf, page_table)
    logf_s, cum_new = _logf_new(f_s, b_f, tot)
    return dict(k_p=k_p, v_p=v_p, kb=kb, vt=vt, logf_p=logf_p[:, :, :H], cum_col=cum_col, cum_row=cum_row,
                k_s=k_s, v_s=v_s, logf_s=logf_s[:, :H].reshape(DB, 1, H), cum_past=cum_past, cum_new=cum_new)


def _logf_new_kernel(f_ref, bf_ref, tot_ref, lf_ref, cn_ref):
    lp = _log_sigmoid(f_ref[...] + bf_ref[...])
    lf_ref[...] = lp
    cn_ref[...] = tot_ref[...] + lp[:, :tot_ref.shape[1]]


def _logf_new(f, b_f_pad, tot):
    DB, L = f.shape
    H = tot.shape[1]
    return pl.pallas_call(
        _logf_new_kernel,
        out_shape=[jax.ShapeDtypeStruct((DB, L), F32), jax.ShapeDtypeStruct((DB, H), F32)],
        compiler_params=pltpu.CompilerParams(vmem_limit_bytes=VMEM_LIMIT),
        name="logf_new",
    )(f, b_f_pad, tot)


def kernel(x_prompt, x_sample, state_rglru_h, state_conv, cache_k, cache_v, cache_logf, page_table,
           w_in, conv_w, conv_b, gate_w, gate_b, rg_lambda, w_out_a, norm_kv, w_kvf, b_f, w_q, w_o,
           norm_mix_pre, norm_mix_post, norm_mlp_pre, norm_mlp_post, w_up, w_down):
    p = dict(w_in=w_in, conv_w=conv_w, conv_b=conv_b, gate_w=gate_w, gate_b=gate_b, rg_lambda=rg_lambda,
             w_out_a=w_out_a, norm_kv=norm_kv, w_kvf=w_kvf, b_f=b_f, w_q=w_q, w_o=w_o,
             norm_mix_pre=norm_mix_pre, norm_mix_post=norm_mix_post, norm_mlp_pre=norm_mlp_pre,
             norm_mlp_post=norm_mlp_post, w_up=w_up, w_down=w_down)
    return _forward(x_prompt, x_sample, state_rglru_h, state_conv, cache_k, cache_v, cache_logf, page_table, p)
```

```python
import functools
import math

import jax
import jax.numpy as jnp
from jax import lax
from jax.experimental import pallas as pl
from jax.experimental.pallas import tpu as pltpu

F32 = jnp.float32
BF16 = jnp.bfloat16

RMS_EPS = 1e-6
LRU_C = 8.0

V7X_VMEM_BYTES = 64 * 1024 * 1024
V7X_LANES = 128
V7X_SUBLANES = 8
VMEM_LIMIT = 62 * 1024 * 1024
MM_SPILL_ALLOWANCE = 5 * 1024 * 1024


def _params(*sem):
    return pltpu.CompilerParams(dimension_semantics=sem, vmem_limit_bytes=VMEM_LIMIT)


def _rms_scale(x):
    return x * lax.rsqrt(jnp.mean(x * x, axis=-1, keepdims=True) + RMS_EPS)


def _log_sigmoid(x):
    return jnp.minimum(x, 0.0) - jnp.log1p(jnp.exp(-jnp.abs(x)))


def _gelu_tanh(x):
    c = math.sqrt(2.0 / math.pi)
    return 0.5 * x * (1.0 + jnp.tanh(c * (x + 0.044715 * (x * x * x))))


def _norm_cast_kernel(x_ref, g_ref, o_ref):
    o_ref[...] = (_rms_scale(x_ref[...]) * g_ref[...]).astype(o_ref.dtype)


def _norm_cast(x, g):
    M, D = x.shape
    tr = min(256, M)
    return pl.pallas_call(
        _norm_cast_kernel,
        grid=(M // tr,),
        in_specs=[pl.BlockSpec((tr, D), lambda i: (i, 0)), pl.BlockSpec((1, D), lambda i: (0, 0))],
        out_specs=pl.BlockSpec((tr, D), lambda i: (i, 0)),
        out_shape=jax.ShapeDtypeStruct((M, D), BF16),
        compiler_params=_params("parallel"),
        name="norm_cast",
    )(x, g.reshape(1, D))


def _resid_norm_kernel(x_ref, m_ref, gp_ref, gn_ref, xo_ref, *n_refs):
    xn = x_ref[...] + _rms_scale(m_ref[...]) * gp_ref[...]
    xo_ref[...] = xn
    if n_refs:
        y = _rms_scale(xn)
        for j, r in enumerate(n_refs):
            r[...] = (y * gn_ref[j:j + 1, :]).astype(r.dtype)


def _resid_norm(x, m, g_post, g_next):
    M, D = x.shape
    n = len(g_next)
    tr = min(256, M)
    gn = jnp.stack(g_next) if n else jnp.zeros((1, D), F32)
    row = pl.BlockSpec((tr, D), lambda i: (i, 0))
    outs = pl.pallas_call(
        _resid_norm_kernel,
        grid=(M // tr,),
        in_specs=[row, row, pl.BlockSpec((1, D), lambda i: (0, 0)),
                  pl.BlockSpec((gn.shape[0], D), lambda i: (0, 0))],
        out_specs=[row] * (1 + n),
        out_shape=[jax.ShapeDtypeStruct((M, D), F32)] + [jax.ShapeDtypeStruct((M, D), BF16)] * n,
        compiler_params=_params("parallel"),
        name="resid_norm",
    )(x, m, g_post.reshape(1, D), gn)
    return outs[0], list(outs[1:])


def _mm_store(acc, o_refs, epilogue):
    if epilogue == "f32":
        o_refs[0][...] = acc
    elif epilogue == "bf16":
        o_refs[0][...] = acc.astype(BF16)
    elif epilogue == "f32_bf16":
        o_refs[0][...] = acc
        o_refs[1][...] = acc.astype(BF16)
    elif epilogue == "f32_bf16t":
        o_refs[0][...] = acc
        o_refs[1][...] = acc.T.astype(BF16)
    elif epilogue == "relu2_bf16":
        h = jnp.maximum(acc, 0.0)
        o_refs[0][...] = (h * h).astype(BF16)
    else:
        raise ValueError(epilogue)


MM_FULLK_TILE = (2048, 512)
MM_KTILED_TILE = (1024, 1024, 2048)


def _fullk_rows(M, K, tn, out_dtypes):
    tm = min(MM_FULLK_TILE[0], M)
    while True:
        need = (2 * tm * K * 2 + 2 * K * tn * 4 + K * tn * 2
                + sum(2 * tm * tn * jnp.dtype(d).itemsize for d in out_dtypes) + MM_SPILL_ALLOWANCE)
        if need <= VMEM_LIMIT or tm % 2 or tm <= V7X_SUBLANES:
            return tm
        tm //= 2

_EPILOGUE_DTYPES = {"f32": (F32,), "bf16": (BF16,), "f32_bf16": (F32, BF16), "f32_bf16t": (F32, BF16),
                    "relu2_bf16": (BF16,)}


def _mm_fullk_kernel(*refs, epilogue, epilogue2):
    n1 = len(_EPILOGUE_DTYPES[epilogue])
    if epilogue2 is None:
        a_ref, w_ref, *o_refs, wb_ref = refs
    else:
        a_ref, a2_ref, w_ref, *o_refs, wb_ref = refs

    @pl.when(pl.program_id(1) == 0)
    def _():
        wb_ref[...] = w_ref[...].astype(BF16)
        if epilogue2 is not None:
            acc2 = jnp.dot(a2_ref[...], wb_ref[...], preferred_element_type=F32)
            _mm_store(acc2, o_refs[n1:], epilogue2)

    acc = jnp.dot(a_ref[...], wb_ref[...], preferred_element_type=F32)
    _mm_store(acc, o_refs[:n1], epilogue)


def _mm_fullk(a, w, layer=0, *, a2=None, n_cols=None, col_block0=0, epilogue="f32", epilogue2=None):
    if w.ndim == 2:
        w = w[None]
    M, K = a.shape
    N = w.shape[2] if n_cols is None else n_cols
    dts = _EPILOGUE_DTYPES[epilogue]
    tn = min(MM_FULLK_TILE[1], N)
    tm = _fullk_rows(M, K, tn, dts)
    assert M % tm == 0 and N % tn == 0
    in_specs = [pl.BlockSpec((tm, K), lambda n, m: (m, 0))]
    out_specs = [pl.BlockSpec((tm, tn), lambda n, m: (m, n))] * len(dts)
    out_shape = [jax.ShapeDtypeStruct((M, N), d) for d in dts]
    if epilogue == "f32_bf16t":
        out_specs[1] = pl.BlockSpec((tn, tm), lambda n, m: (n, m))
        out_shape[1] = jax.ShapeDtypeStruct((N, M), dts[1])
    args = [a]
    if a2 is not None:
        epilogue2 = epilogue2 or epilogue
        M2 = a2.shape[0]
        dts2 = _EPILOGUE_DTYPES[epilogue2]
        in_specs.append(pl.BlockSpec((M2, K), lambda n, m: (0, 0)))
        out_specs += [pl.BlockSpec((M2, tn), lambda n, m: (0, n))] * len(dts2)
        out_shape += [jax.ShapeDtypeStruct((M2, N), d) for d in dts2]
        args.append(a2)
    in_specs.append(pl.BlockSpec((None, K, tn), lambda n, m: (layer, 0, n + col_block0)))
    outs = pl.pallas_call(
        functools.partial(_mm_fullk_kernel, epilogue=epilogue, epilogue2=epilogue2 if a2 is not None else None),
        grid=(N // tn, M // tm),
        in_specs=in_specs,
        out_specs=out_specs,
        out_shape=out_shape,
        scratch_shapes=[pltpu.VMEM((K, tn), BF16)],
        compiler_params=_params("arbitrary", "arbitrary"),
        name="mm_fullk_" + epilogue,
    )(*args, w)
    unpack = lambda o: o if len(o) > 1 else o[0]
    if a2 is None:
        return unpack(outs)
    return unpack(outs[:len(dts)]), unpack(outs[len(dts):])


def _mm_ktiled_kernel(*refs, with2):
    if with2:
        a_ref, a2_ref, w_ref, o_ref, o2_ref = refs
    else:
        a_ref, w_ref, o_ref = refs
    first_k = pl.program_id(2) == 0
    wb = w_ref[...].astype(BF16)
    part = jnp.dot(a_ref[...], wb, preferred_element_type=F32)

    @pl.when(first_k)
    def _():
        o_ref[...] = part

    @pl.when(jnp.logical_not(first_k))
    def _():
        o_ref[...] += part

    if with2:
        first_m = pl.program_id(0) == 0

        @pl.when(first_m & first_k)
        def _():
            o2_ref[...] = jnp.dot(a2_ref[...], wb, preferred_element_type=F32)

        @pl.when(first_m & jnp.logical_not(first_k))
        def _():
            o2_ref[...] += jnp.dot(a2_ref[...], wb, preferred_element_type=F32)


def _mm_ktiled(a, w, layer, a2=None):
    M, K = a.shape
    N = w.shape[2]
    tm = min(MM_KTILED_TILE[0], M)
    tn = min(MM_KTILED_TILE[1], N)
    tk = min(MM_KTILED_TILE[2], K)
    assert M % tm == 0 and N % tn == 0 and K % tk == 0
    in_specs = [pl.BlockSpec((tm, tk), lambda m, n, k: (m, k))]
    out_specs = [pl.BlockSpec((tm, tn), lambda m, n, k: (m, n))]
    out_shape = [jax.ShapeDtypeStruct((M, N), F32)]
    args = [a]
    if a2 is not None:
        M2 = a2.shape[0]
        last_n, last_k = N // tn - 1, K // tk - 1
        in_specs.append(pl.BlockSpec((M2, tk), lambda m, n, k: (0, jnp.where(m == 0, k, last_k))))
        out_specs.append(pl.BlockSpec((M2, tn), lambda m, n, k: (0, jnp.where(m == 0, n, last_n))))
        out_shape.append(jax.ShapeDtypeStruct((M2, N), F32))
        args.append(a2)
    in_specs.append(pl.BlockSpec((None, tk, tn), lambda m, n, k: (layer, k, n)))
    outs = pl.pallas_call(
        functools.partial(_mm_ktiled_kernel, with2=a2 is not None),
        grid=(M // tm, N // tn, K // tk),
        in_specs=in_specs,
        out_specs=out_specs,
        out_shape=out_shape,
        compiler_params=_params("arbitrary", "arbitrary", "arbitrary"),
        name="mm_ktiled",
    )(*args, w)
    return outs[0] if a2 is None else (outs[0], outs[1])


def _rglru_coeffs(conv, gw_ref, gb_ref, lam_ref):
    cb = conv.astype(BF16)
    gi = jnp.dot(cb, gw_ref[0, 0].astype(BF16), preferred_element_type=F32) + gb_ref[0:1, :]
    gr = jnp.dot(cb, gw_ref[1, 0].astype(BF16), preferred_element_type=F32) + gb_ref[1:2, :]
    i_gate = jax.nn.sigmoid(gi)
    r_gate = jax.nn.sigmoid(gr)
    log_a = LRU_C * r_gate * _log_sigmoid(lam_ref[...])
    a = jnp.exp(log_a)
    th = jnp.tanh(log_a)
    mult = jnp.sqrt(-2.0 * th / (1.0 - th))
    return a, mult * i_gate * conv


def _rglru_seq_kernel(x_ref, y_ref, h0_ref, c0_ref, cw_ref, cb_ref, gw_ref, gb_ref, lam_ref,
                      hg_ref, hl_ref, xs_ref, hc_ref, *, tT):
    t = pl.program_id(2)
    S = V7X_SUBLANES

    @pl.when(t == 0)
    def _():
        xs_ref[0:S, :] = c0_ref[0]
        hc_ref[...] = h0_ref[0]

    x = x_ref[0]
    xs_ref[S:S + tT, :] = x
    cw = cw_ref[...]
    nw = cw.shape[0]
    conv = cb_ref[...] + cw[nw - 1:nw, :] * x
    for j in range(1, nw):
        conv = conv + cw[nw - 1 - j:nw - j, :] * xs_ref[S - j:S - j + tT, :]
    xs_ref[0:S, :] = xs_ref[tT:tT + S, :]

    a, b = _rglru_coeffs(conv, gw_ref, gb_ref, lam_ref)

    row = lax.broadcasted_iota(jnp.int32, (S, a.shape[1]), 0)
    h = hc_ref[...]
    hs = []
    for r in range(0, tT, S):
        a8 = a[r:r + S]
        b8 = b[r:r + S]
        s = 1
        while s < S:
            keep = row >= s
            b8 = jnp.where(keep, a8 * pltpu.roll(b8, s, 0) + b8, b8)
            a8 = jnp.where(keep, a8 * pltpu.roll(a8, s, 0), a8)
            s *= 2
        h8 = a8 * h + b8
        hs.append(h8)
        h = h8[S - 1:S, :]
    hc_ref[...] = h
    hl_ref[0] = h
    hg_ref[0] = (jnp.concatenate(hs, axis=0) * _gelu_tanh(y_ref[0])).astype(hg_ref.dtype)


def _rglru_seq(xy, h0, conv0, conv_w, conv_b, gate_w, gate_b, lam):
    B, T, C2 = xy.shape
    C = C2 // 2
    nblk, cblk = gate_w.shape[1], gate_w.shape[2]
    nw = conv_w.shape[0]
    S = V7X_SUBLANES
    tT = min(512, T)
    assert T % tT == 0 and tT % S == 0 and nw - 1 <= S and cblk * nblk == C
    c0 = jnp.concatenate([jnp.zeros((B, S - (nw - 1), C), F32), conv0], axis=1)
    hg, hl = pl.pallas_call(
        functools.partial(_rglru_seq_kernel, tT=tT),
        grid=(B, nblk, T // tT),
        in_specs=[
            pl.BlockSpec((1, tT, cblk), lambda b, n, t: (b, t, n)),
            pl.BlockSpec((1, tT, cblk), lambda b, n, t: (b, t, n + nblk)),
            pl.BlockSpec((1, 1, cblk), lambda b, n, t: (b, 0, n)),
            pl.BlockSpec((1, S, cblk), lambda b, n, t: (b, 0, n)),
            pl.BlockSpec((nw, cblk), lambda b, n, t: (0, n)),
            pl.BlockSpec((1, cblk), lambda b, n, t: (0, n)),
            pl.BlockSpec((2, 1, cblk, cblk), lambda b, n, t: (0, n, 0, 0)),
            pl.BlockSpec((2, cblk), lambda b, n, t: (0, n)),
            pl.BlockSpec((1, cblk), lambda b, n, t: (0, n)),
        ],
        out_specs=[pl.BlockSpec((1, tT, cblk), lambda b, n, t: (b, t, n)),
                   pl.BlockSpec((1, 1, cblk), lambda b, n, t: (b, 0, n))],
        out_shape=[jax.ShapeDtypeStruct((B, T, C), BF16), jax.ShapeDtypeStruct((B, 1, C), F32)],
        scratch_shapes=[pltpu.VMEM((tT + S, cblk), F32), pltpu.VMEM((1, cblk), F32)],
        compiler_params=_params("parallel", "parallel", "arbitrary"),
        name="rglru_seq",
    )(xy, xy, h0.reshape(B, 1, C), c0, conv_w, conv_b.reshape(1, C), gate_w, gate_b, lam.reshape(1, C))
    return hg, hl.reshape(B, C)


def _rglru_step_kernel(x_ref, y_ref, h0_ref, c0_ref, cw_ref, cb_ref, gw_ref, gb_ref, lam_ref,
                       hg_ref, hl_ref):
    x = x_ref[...]
    cw = cw_ref[...]
    nw = cw.shape[0]
    conv = cb_ref[...] + cw[nw - 1:nw, :] * x
    for j in range(1, nw):
        conv = conv + cw[nw - 1 - j:nw - j, :] * c0_ref[nw - 1 - j]
    a, b = _rglru_coeffs(conv, gw_ref, gb_ref, lam_ref)
    h = a * h0_ref[...] + b
    hl_ref[...] = h
    hg_ref[...] = (h * _gelu_tanh(y_ref[...])).astype(hg_ref.dtype)


def _rglru_step(xy, h0, conv0, conv_w, conv_b, gate_w, gate_b, lam):
    B, C2 = xy.shape
    C = C2 // 2
    nblk, cblk = gate_w.shape[1], gate_w.shape[2]
    nw = conv_w.shape[0]
    c0 = jnp.swapaxes(conv0, 0, 1)
    return pl.pallas_call(
        _rglru_step_kernel,
        grid=(nblk,),
        in_specs=[
            pl.BlockSpec((B, cblk), lambda n: (0, n)),
            pl.BlockSpec((B, cblk), lambda n: (0, n + nblk)),
            pl.BlockSpec((B, cblk), lambda n: (0, n)),
            pl.BlockSpec((nw - 1, B, cblk), lambda n: (0, 0, n)),
            pl.BlockSpec((nw, cblk), lambda n: (0, n)),
            pl.BlockSpec((1, cblk), lambda n: (0, n)),
            pl.BlockSpec((2, 1, cblk, cblk), lambda n: (0, n, 0, 0)),
            pl.BlockSpec((2, cblk), lambda n: (0, n)),
            pl.BlockSpec((1, cblk), lambda n: (0, n)),
        ],
        out_specs=[pl.BlockSpec((B, cblk), lambda n: (0, n)), pl.BlockSpec((B, cblk), lambda n: (0, n))],
        out_shape=[jax.ShapeDtypeStruct((B, C), BF16), jax.ShapeDtypeStruct((B, C), F32)],
        compiler_params=_params("parallel"),
        name="rglru_step",
    )(xy, xy, h0, c0, conv_w, conv_b.reshape(1, C), gate_w, gate_b, lam.reshape(1, C))


def _tri_dots(lp, carry_col_ref, carry_row_ref, cum_col_ref, cum_row_ref):
    P = lp.shape[0]
    hi = lax.Precision.HIGHEST
    r = lax.broadcasted_iota(jnp.int32, (P, P), 0)
    c = lax.broadcasted_iota(jnp.int32, (P, P), 1)
    ones = jnp.ones((P, P), F32)
    if cum_col_ref is not None:
        tril = (c <= r).astype(F32)
        cum_col_ref[0] = jnp.dot(tril, lp, precision=hi, preferred_element_type=F32) + carry_col_ref[...]
        carry_col_ref[...] += jnp.dot(ones, lp, precision=hi, preferred_element_type=F32)
    lpt = lp.T
    triu = (r <= c).astype(F32)
    cum_row_ref[0] = jnp.dot(lpt, triu, precision=hi, preferred_element_type=F32) + carry_row_ref[...]
    carry_row_ref[...] += jnp.dot(lpt, ones, precision=hi, preferred_element_type=F32)


def _gate_logits(f, n_heads):
    return jnp.where(lax.broadcasted_iota(jnp.int32, f.shape, f.ndim - 1) < n_heads, f, 0.0)


def _logf_cum_kernel(f_ref, bf_ref, lf_ref, cc_ref, cr_ref, car_c, car_r, *, n_heads):
    @pl.when(pl.program_id(1) == 0)
    def _():
        car_c[...] = jnp.zeros_like(car_c)
        car_r[...] = jnp.zeros_like(car_r)

    lp = _log_sigmoid(_gate_logits(f_ref[0], n_heads) + bf_ref[...])
    lf_ref[0] = lp
    _tri_dots(lp, car_c, car_r, cc_ref, cr_ref)


def _logf_cum(f, b_f_pad, n_heads):
    B, T, L = f.shape
    P = V7X_LANES
    assert T % P == 0 and L == P
    return pl.pallas_call(
        functools.partial(_logf_cum_kernel, n_heads=n_heads),
        grid=(B, T // P),
        in_specs=[pl.BlockSpec((1, P, L), lambda b, t: (b, t, 0)), pl.BlockSpec((1, L), lambda b, t: (0, 0))],
        out_specs=[pl.BlockSpec((1, P, L), lambda b, t: (b, t, 0)),
                   pl.BlockSpec((1, P, L), lambda b, t: (b, t, 0)),
                   pl.BlockSpec((1, L, P), lambda b, t: (b, 0, t))],
        out_shape=[jax.ShapeDtypeStruct((B, T, L), F32), jax.ShapeDtypeStruct((B, T, L), F32),
                   jax.ShapeDtypeStruct((B, L, T), F32)],
        scratch_shapes=[pltpu.VMEM((P, L), F32), pltpu.VMEM((L, P), F32)],
        compiler_params=_params("parallel", "arbitrary"),
        name="logf_cum",
    )(f, b_f_pad)


def _paged_cum_kernel(pt_ref, *refs, n_in):
    lf_refs = refs[:n_in]
    cc_ref, tot_ref, car = refs[n_in:]

    @pl.when(pl.program_id(1) == 0)
    def _():
        car[...] = jnp.zeros_like(car)

    P = lf_refs[0].shape[1]
    hi = lax.Precision.HIGHEST
    tril = (lax.broadcasted_iota(jnp.int32, (P, P), 1) <= lax.broadcasted_iota(jnp.int32, (P, P), 0)).astype(F32)
    ones = jnp.ones((P, P), F32)
    for g, lf_ref in enumerate(lf_refs):
        lp = lf_ref[0]
        cc_ref[0, g * P:(g + 1) * P, :] = jnp.dot(tril, lp, precision=hi, preferred_element_type=F32) + car[...]
        car[...] += jnp.dot(ones, lp, precision=hi, preferred_element_type=F32)
    tot_ref[0] = car[0:V7X_SUBLANES, :]


def _paged_cum(cache_logf, page_table):
    n_pool, P, H = cache_logf.shape
    DB, n_pages = page_table.shape
    G = math.gcd(n_pages, 8)
    S = V7X_SUBLANES
    specs = [pl.BlockSpec((1, P, H), lambda b, s, pt, g=g: (pt[b, s * G + g], 0, 0)) for g in range(G)]
    cum, tot = pl.pallas_call(
        functools.partial(_paged_cum_kernel, n_in=G),
        grid_spec=pltpu.PrefetchScalarGridSpec(
            num_scalar_prefetch=1,
            grid=(DB, n_pages // G),
            in_specs=specs,
            out_specs=[pl.BlockSpec((1, G * P, H), lambda b, s, pt: (b, s, 0)),
                       pl.BlockSpec((1, S, H), lambda b, s, pt: (b, 0, 0))],
            scratch_shapes=[pltpu.VMEM((P, H), F32)],
        ),
        out_shape=[jax.ShapeDtypeStruct((DB, n_pages * P, H), F32), jax.ShapeDtypeStruct((DB, S, H), F32)],
        compiler_params=_params("parallel", "arbitrary"),
        name="paged_cum",
    )(page_table, *([cache_logf] * G))
    return cum, tot[:, 0, :]


def _fox_prompt_kernel(q_ref, k_ref, vt_ref, cq_ref, ck_ref, o_ref, *, tq, hd, hb, nq, scale):
    qi = pl.program_id(2)
    nt = (((1,), (1,)), ((), ()))
    below_or_on = (lax.broadcasted_iota(jnp.int32, (tq, 1), 0)
                   <= lax.broadcasted_iota(jnp.int32, (1, tq), 1))

    def q_block(c):
        lo = c * tq
        for j in range(hb):
            cols = slice(j * hd, (j + 1) * hd)
            qj = q_ref[0, :, cols]
            cqj = cq_ref[0, 0, j:j + 1, :]
            sd = lax.dot_general(k_ref[0, lo:lo + tq, cols], qj, nt, preferred_element_type=F32) * scale
            sd = sd + (cqj - ck_ref[0, 0, lo:lo + tq, j:j + 1])
            sd = jnp.where(below_or_on, sd, -jnp.inf)
            m = jnp.max(sd, axis=0, keepdims=True)
            if c:
                sp = lax.dot_general(k_ref[0, 0:lo, cols], qj, nt, preferred_element_type=F32) * scale
                sp = sp + (cqj - ck_ref[0, 0, 0:lo, j:j + 1])
                m = jnp.maximum(m, jnp.max(sp, axis=0, keepdims=True))
            pd = jnp.exp(sd - m)
            l = jnp.sum(pd, axis=0, keepdims=True)
            acc = jnp.dot(vt_ref[cols, lo:lo + tq], pd.astype(BF16), preferred_element_type=F32)
            if c:
                pp = jnp.exp(sp - m)
                l = l + jnp.sum(pp, axis=0, keepdims=True)
                acc = acc + jnp.dot(vt_ref[cols, 0:lo], pp.astype(BF16), preferred_element_type=F32)
            o_ref[0, :, cols] = (acc / l).T.astype(o_ref.dtype)

    for c in range(nq):
        pl.when(qi == c)(functools.partial(q_block, c))


def _fox_prompt(q, k, vt, cum_col, cum_row, n_heads):
    B, T, D = q.shape
    hd = D // n_heads
    hb = min(4, n_heads)
    hg = n_heads // hb
    tq = min(256, T)
    assert n_heads % hb == 0 and T % tq == 0
    cq = cum_row[:, :n_heads, :].reshape(B, hg, hb, T)
    ck = jnp.transpose(cum_col[:, :, :n_heads].reshape(B, T, hg, hb), (0, 2, 1, 3))
    return pl.pallas_call(
        functools.partial(_fox_prompt_kernel, tq=tq, hd=hd, hb=hb, nq=T // tq, scale=hd ** -0.5),
        grid=(B, hg, T // tq),
        in_specs=[pl.BlockSpec((1, tq, hb * hd), lambda b, g, i: (b, i, g)),
                  pl.BlockSpec((1, T, hb * hd), lambda b, g, i: (b, 0, g)),
                  pl.BlockSpec((hb * hd, T), lambda b, g, i: (g, b)),
                  pl.BlockSpec((1, 1, hb, tq), lambda b, g, i: (b, g, 0, i)),
                  pl.BlockSpec((1, 1, T, hb), lambda b, g, i: (b, g, 0, 0))],
        out_specs=pl.BlockSpec((1, tq, hb * hd), lambda b, g, i: (b, i, g)),
        out_shape=jax.ShapeDtypeStruct((B, T, D), BF16),
        compiler_params=_params("parallel", "parallel", "arbitrary"),
        name="fox_prompt",
    )(q, k, vt, cq, ck)


DECODE_PAGES_PER_STEP = 4


def _fox_decode_kernel(pt_ref, q_ref, *refs, n_pg, scale):
    k_refs, v_refs = refs[:n_pg], refs[n_pg:2 * n_pg]
    ckf_ref, cqf_ref, cqb_ref, kn_ref, vn_ref, o_ref, m_ref, l_ref, acc_ref = refs[2 * n_pg:]
    p_idx = pl.program_id(1)
    H = q_ref.shape[1]

    @pl.when(p_idx == 0)
    def _():
        m_ref[...] = jnp.full_like(m_ref, -jnp.inf)
        l_ref[...] = jnp.zeros_like(l_ref)
        acc_ref[...] = jnp.zeros_like(acc_ref)

    qb = q_ref[0].astype(BF16)
    rows = k_refs[0].shape[1]
    own = (lax.broadcasted_iota(jnp.int32, (H, rows), 1) % H) == lax.broadcasted_iota(jnp.int32, (H, rows), 0)
    m, l, acc = m_ref[...], l_ref[...], acc_ref[...]
    for g in range(n_pg):
        kb = k_refs[g][0].astype(BF16)
        s = lax.dot_general(qb, kb, (((1,), (1,)), ((), ())), preferred_element_type=F32) * scale
        s = s + (cqf_ref[0] - ckf_ref[0, 0, g:g + 1, :])
        s = jnp.where(own, s, -jnp.inf)
        m_new = jnp.maximum(m, jnp.max(s, axis=1, keepdims=True))
        alpha = jnp.exp(m - m_new)
        p = jnp.exp(s - m_new[:, 0:1])
        l = alpha * l + jnp.sum(p, axis=1, keepdims=True)
        acc = alpha * acc + jnp.dot(p.astype(BF16), v_refs[g][0].astype(BF16), preferred_element_type=F32)
        m = m_new
    m_ref[...] = m
    l_ref[...] = l
    acc_ref[...] = acc

    @pl.when(p_idx == pl.num_programs(1) - 1)
    def _():
        qn = q_ref[0].astype(BF16).astype(F32)
        kn = kn_ref[0].astype(BF16).astype(F32)
        vn = vn_ref[0].astype(BF16).astype(F32)
        cq = cqb_ref[0]
        s_n = jnp.sum(qn * kn, axis=1, keepdims=True) * scale + (cq - cq)
        m_p = m_ref[...]
        m_f = jnp.maximum(m_p, s_n)
        al = jnp.exp(m_p - m_f)
        p_n = jnp.exp(s_n - m_f)
        l_f = al * l_ref[...] + p_n
        o = al * acc_ref[...] + p_n.astype(BF16).astype(F32) * vn
        o_ref[0] = (o / l_f).astype(o_ref.dtype)


def _fox_decode(q, k_new, v_new, cache_k, cache_v, page_table, cum_past, cum_new):
    DB, H, hd = q.shape
    n_pool, P = cache_k.shape[0], cache_k.shape[1]
    n_pages = page_table.shape[1]
    R = P * H
    assert hd == V7X_LANES
    k2 = cache_k.reshape(n_pool, R, hd)
    v2 = cache_v.reshape(n_pool, R, hd)
    G = math.gcd(n_pages, DECODE_PAGES_PER_STEP)
    ckf = cum_past.reshape(DB, n_pages // G, G, R)
    cqf = jnp.tile(cum_new, (1, P)).reshape(DB, 1, R)
    cqb = jnp.broadcast_to(cum_new[:, :, None], (DB, H, hd))
    per_seq = lambda shape: pl.BlockSpec((1,) + shape, lambda b, s, pt: (b, 0, 0))
    page = [pl.BlockSpec((1, R, hd), lambda b, s, pt, g=g: (pt[b, s * G + g], 0, 0)) for g in range(G)]
    return pl.pallas_call(
        functools.partial(_fox_decode_kernel, n_pg=G, scale=hd ** -0.5),
        grid_spec=pltpu.PrefetchScalarGridSpec(
            num_scalar_prefetch=1,
            grid=(DB, n_pages // G),
            in_specs=[per_seq((H, hd))] + page + page
                     + [pl.BlockSpec((1, 1, G, R), lambda b, s, pt: (b, s, 0, 0)),
                        per_seq((1, R)), per_seq((H, hd)), per_seq((H, hd)), per_seq((H, hd))],
            out_specs=per_seq((H, hd)),
            scratch_shapes=[pltpu.VMEM((H, hd), F32), pltpu.VMEM((H, hd), F32), pltpu.VMEM((H, hd), F32)],
        ),
        out_shape=jax.ShapeDtypeStruct((DB, H, hd), BF16),
        compiler_params=_params("parallel", "arbitrary"),
        name="fox_decode",
    )(page_table, q, *([k2] * G), *([v2] * G), ckf, cqf, cqb, k_new, v_new)


def _forward(x_prompt, x_sample, h0_s, conv0_s, cache_k, cache_v, cache_logf, page_table, p):
    B, T, D = x_prompt.shape
    DB, DT, _ = x_sample.shape
    assert DT == 1, "the sample group carries one new token per sequence"
    M = B * T
    depth = p["w_up"].shape[0]
    n_a = p["w_in"].shape[0]
    H = p["b_f"].shape[0]
    hd = p["w_q"].shape[2] // H
    d_attn = H * hd
    nw = p["conv_w"].shape[1]
    assert T >= nw - 1

    xp = x_prompt.reshape(M, D)
    xs = x_sample.reshape(DB, D)
    xnp = _norm_cast(xp, p["norm_mix_pre"][0])
    xns = _norm_cast(xs, p["norm_mix_pre"][0])
    hs_p, convs_p, hs_s, convs_s = [], [], [], []
    kv = None
    for l in range(depth):
        if l < n_a:
            xy_p, xy_s = _mm_fullk(xnp, p["w_in"], l, a2=xns, epilogue="f32")
            C = xy_p.shape[1] // 2
            args = (p["conv_w"][l], p["conv_b"][l], p["gate_w"][l], p["gate_b"][l], p["rg_lambda"][l])
            xy_p3 = xy_p.reshape(B, T, 2 * C)
            hg_p, h_p = _rglru_seq(xy_p3, jnp.zeros((B, C), F32), jnp.zeros((B, nw - 1, C), F32), *args)
            hg_s, h_s = _rglru_step(xy_s, h0_s[l], conv0_s[l], *args)
            hs_p.append(h_p)
            hs_s.append(h_s)
            convs_p.append(xy_p3[:, T - (nw - 1):, :C])
            convs_s.append(jnp.concatenate([conv0_s[l], xy_s[:, None, :C]], axis=1)[:, 1:])
            m_p, m_s = _mm_fullk(hg_p.reshape(M, C), p["w_out_a"], l, a2=hg_s, epilogue="f32")
        else:
            j = l - n_a
            q_p, q_s = _mm_fullk(xnp, p["w_q"], j, a2=xns, epilogue="bf16", epilogue2="f32")
            o_p = _fox_prompt(q_p.reshape(B, T, d_attn), kv["kb"].reshape(B, T, d_attn), kv["vt"],
                              kv["cum_col"], kv["cum_row"], H)
            o_s = _fox_decode(q_s.reshape(DB, H, hd), kv["k_s"].reshape(DB, H, hd), kv["v_s"].reshape(DB, H, hd),
                              cache_k, cache_v, page_table, kv["cum_past"], kv["cum_new"])
            m_p, m_s = _mm_fullk(o_p.reshape(M, d_attn), p["w_o"], j, a2=o_s.reshape(DB, d_attn), epilogue="f32")
        xp, (xn2p,) = _resid_norm(xp, m_p, p["norm_mix_post"][l], [p["norm_mlp_pre"][l]])
        xs, (xn2s,) = _resid_norm(xs, m_s, p["norm_mix_post"][l], [p["norm_mlp_pre"][l]])
        hmid_p, hmid_s = _mm_fullk(xn2p, p["w_up"], l, a2=xn2s, epilogue="relu2_bf16")
        f_p, f_s = _mm_ktiled(hmid_p, p["w_down"], l, a2=hmid_s)
        if l + 1 == n_a:
            gains = [p["norm_kv"], p["norm_mix_pre"][l + 1]]
            xp, (xkv_p, xnp) = _resid_norm(xp, f_p, p["norm_mlp_post"][l], gains)
            xs, (xkv_s, xns) = _resid_norm(xs, f_s, p["norm_mlp_post"][l], gains)
            kv = _shared_kv(xkv_p, xkv_s, B, T, H, hd, cache_logf, page_table, p)
        elif l + 1 < depth:
            xp, (xnp,) = _resid_norm(xp, f_p, p["norm_mlp_post"][l], [p["norm_mix_pre"][l + 1]])
            xs, (xns,) = _resid_norm(xs, f_s, p["norm_mlp_post"][l], [p["norm_mix_pre"][l + 1]])
        else:
            xp, _ = _resid_norm(xp, f_p, p["norm_mlp_post"][l], [])
            xs, _ = _resid_norm(xs, f_s, p["norm_mlp_post"][l], [])
    return (xp.reshape(B, T, D), xs.reshape(DB, DT, D),
            jnp.stack(hs_p), jnp.stack(convs_p),
            kv["k_p"].reshape(B, T, H, hd), kv["v_p"].reshape(B, T, H, hd), kv["logf_p"],
            jnp.stack(hs_s), jnp.stack(convs_s),
            kv["k_s"].reshape(DB, DT, H, hd), kv["v_s"].reshape(DB, DT, H, hd), kv["logf_s"])


def _shared_kv(xkv_p, xkv_s, B, T, H, hd, cache_logf, page_table, p):
    d_attn = H * hd
    DB = xkv_s.shape[0]
    L = V7X_LANES
    w_kvf = p["w_kvf"]
    tn = min(512, d_attn)
    (k_p, kb), k_s = _mm_fullk(xkv_p, w_kvf, a2=xkv_s, n_cols=d_attn, col_block0=0,
                               epilogue="f32_bf16", epilogue2="f32")
    (v_p, vt), v_s = _mm_fullk(xkv_p, w_kvf, a2=xkv_s, n_cols=d_attn, col_block0=d_attn // tn,
                               epilogue="f32_bf16t", epilogue2="f32")
    assert (2 * d_attn) % L == 0 and H <= L
    b_f = jnp.pad(p["b_f"], (0, L - H)).reshape(1, L)
    f_p, f_s = _mm_fullk(xkv_p, w_kvf, a2=xkv_s, n_cols=L, col_block0=2 * d_attn // L, epilogue="f32")
    logf_p, cum_col, cum_row = _logf_cum(f_p.reshape(B, T, L), b_f, H)
    cum_past, tot = _paged_cum(cache_logf, page_table)
    logf_s, cum_new = _logf_new(f_s, b_f, tot)
    return dict(k_p=k_p, v_p=v_p, kb=kb, vt=vt, logf_p=logf_p[:, :, :H], cum_col=cum_col, cum_row=cum_row,
                k_s=k_s, v_s=v_s, logf_s=logf_s[:, :H].reshape(DB, 1, H), cum_past=cum_past, cum_new=cum_new)


def _logf_new_kernel(f_ref, bf_ref, tot_ref, lf_ref, cn_ref):
    H = tot_ref.shape[1]
    lp = _log_sigmoid(_gate_logits(f_ref[...], H) + bf_ref[...])
    lf_ref[...] = lp
    cn_ref[...] = tot_ref[...] + lp[:, :H]


def _logf_new(f, b_f_pad, tot):
    DB, L = f.shape
    H = tot.shape[1]
    return pl.pallas_call(
        _logf_new_kernel,
        out_shape=[jax.ShapeDtypeStruct((DB, L), F32), jax.ShapeDtypeStruct((DB, H), F32)],
        compiler_params=pltpu.CompilerParams(vmem_limit_bytes=VMEM_LIMIT),
        name="logf_new",
    )(f, b_f_pad, tot)


def kernel(x_prompt, x_sample, state_rglru_h, state_conv, cache_k, cache_v, cache_logf, page_table,
           w_in, conv_w, conv_b, gate_w, gate_b, rg_lambda, w_out_a, norm_kv, w_kvf, b_f, w_q, w_o,
           norm_mix_pre, norm_mix_post, norm_mlp_pre, norm_mlp_post, w_up, w_down):
    p = dict(w_in=w_in, conv_w=conv_w, conv_b=conv_b, gate_w=gate_w, gate_b=gate_b, rg_lambda=rg_lambda,
             w_out_a=w_out_a, norm_kv=norm_kv, w_kvf=w_kvf, b_f=b_f, w_q=w_q, w_o=w_o,
             norm_mix_pre=norm_mix_pre, norm_mix_post=norm_mix_post, norm_mlp_pre=norm_mlp_pre,
             norm_mlp_post=norm_mlp_post, w_up=w_up, w_down=w_down)
    return _forward(x_prompt, x_sample, state_rglru_h, state_conv, cache_k, cache_v, cache_logf, page_table, p)
```

```python
import functools
import math

import jax
import jax.numpy as jnp
from jax import lax
from jax.experimental import pallas as pl
from jax.experimental.pallas import tpu as pltpu

F32 = jnp.float32
BF16 = jnp.bfloat16

RMS_EPS = 1e-6
LRU_C = 8.0
LOG2E = math.log2(math.e)

V7X_VMEM_BYTES = 64 * 1024 * 1024
V7X_LANES = 128
V7X_SUBLANES = 8
VMEM_LIMIT = 62 * 1024 * 1024
MM_SPILL_ALLOWANCE = 5 * 1024 * 1024


def _params(*sem):
    return pltpu.CompilerParams(dimension_semantics=sem, vmem_limit_bytes=VMEM_LIMIT)


def _rms_scale(x):
    return x * lax.rsqrt(jnp.mean(x * x, axis=-1, keepdims=True) + RMS_EPS)


def _log_sigmoid(x):
    return jnp.minimum(x, 0.0) - jnp.log1p(jnp.exp(-jnp.abs(x)))


def _gelu_tanh(x):
    c = math.sqrt(2.0 / math.pi)
    return 0.5 * x * (1.0 + jnp.tanh(c * (x + 0.044715 * (x * x * x))))


def _norm_cast_kernel(x_ref, g_ref, o_ref):
    o_ref[...] = (_rms_scale(x_ref[...]) * g_ref[...]).astype(o_ref.dtype)


def _norm_cast(x, g):
    M, D = x.shape
    tr = min(256, M)
    return pl.pallas_call(
        _norm_cast_kernel,
        grid=(M // tr,),
        in_specs=[pl.BlockSpec((tr, D), lambda i: (i, 0)), pl.BlockSpec((1, D), lambda i: (0, 0))],
        out_specs=pl.BlockSpec((tr, D), lambda i: (i, 0)),
        out_shape=jax.ShapeDtypeStruct((M, D), BF16),
        compiler_params=_params("parallel"),
        name="norm_cast",
    )(x, g.reshape(1, D))


RESID_ROWS = 256


def _resid_norm_kernel(x_ref, m_ref, gp_ref, gn_ref, xo_ref, *n_refs):
    xn = x_ref[...] + _rms_scale(m_ref[...]) * gp_ref[...]
    xo_ref[...] = xn
    if n_refs:
        y = _rms_scale(xn)
        for j, r in enumerate(n_refs):
            r[...] = (y * gn_ref[j:j + 1, :]).astype(r.dtype)


def _resid_norm(x, m, g_post, g_next):
    M, D = x.shape
    n = len(g_next)
    tr = min(RESID_ROWS, M)
    gn = jnp.stack(g_next) if n else jnp.zeros((1, D), F32)
    row = pl.BlockSpec((tr, D), lambda i: (i, 0))
    outs = pl.pallas_call(
        _resid_norm_kernel,
        grid=(M // tr,),
        in_specs=[row, row, pl.BlockSpec((1, D), lambda i: (0, 0)),
                  pl.BlockSpec((gn.shape[0], D), lambda i: (0, 0))],
        out_specs=[row] * (1 + n),
        out_shape=[jax.ShapeDtypeStruct((M, D), F32)] + [jax.ShapeDtypeStruct((M, D), BF16)] * n,
        compiler_params=_params("parallel"),
        name="resid_norm",
    )(x, m, g_post.reshape(1, D), gn)
    return outs[0], list(outs[1:])


def _mm_store(acc, o_refs, epilogue):
    if epilogue == "f32":
        o_refs[0][...] = acc
    elif epilogue == "bf16":
        o_refs[0][...] = acc.astype(BF16)
    elif epilogue == "f32_bf16":
        o_refs[0][...] = acc
        o_refs[1][...] = acc.astype(BF16)
    elif epilogue == "f32_bf16t":
        o_refs[0][...] = acc
        o_refs[1][...] = acc.T.astype(BF16)
    elif epilogue == "relu2_bf16":
        h = jnp.maximum(acc, 0.0)
        o_refs[0][...] = (h * h).astype(BF16)
    else:
        raise ValueError(epilogue)


MM_FULLK_TILE = (2048, 512)
MM_KTILED_TILE = (1024, 1024, 2048)


def _fullk_rows(M, K, tn, out_dtypes):
    tm = min(MM_FULLK_TILE[0], M)
    while True:
        need = (2 * tm * K * 2 + 2 * K * tn * 4 + K * tn * 2
                + sum(2 * tm * tn * jnp.dtype(d).itemsize for d in out_dtypes) + MM_SPILL_ALLOWANCE)
        if need <= VMEM_LIMIT or tm % 2 or tm <= V7X_SUBLANES:
            return tm
        tm //= 2

_EPILOGUE_DTYPES = {"f32": (F32,), "bf16": (BF16,), "f32_bf16": (F32, BF16), "f32_bf16t": (F32, BF16),
                    "relu2_bf16": (BF16,)}


def _mm_fullk_kernel(*refs, epilogue, epilogue2):
    n1 = len(_EPILOGUE_DTYPES[epilogue])
    if epilogue2 is None:
        a_ref, w_ref, *o_refs, wb_ref = refs
    else:
        a_ref, a2_ref, w_ref, *o_refs, wb_ref = refs

    @pl.when(pl.program_id(1) == 0)
    def _():
        wb_ref[...] = w_ref[...].astype(BF16)
        if epilogue2 is not None:
            acc2 = jnp.dot(a2_ref[...], wb_ref[...], preferred_element_type=F32)
            _mm_store(acc2, o_refs[n1:], epilogue2)

    acc = jnp.dot(a_ref[...], wb_ref[...], preferred_element_type=F32)
    _mm_store(acc, o_refs[:n1], epilogue)


def _mm_fullk(a, w, layer=0, *, a2=None, n_cols=None, col_block0=0, epilogue="f32", epilogue2=None):
    if w.ndim == 2:
        w = w[None]
    M, K = a.shape
    N = w.shape[2] if n_cols is None else n_cols
    dts = _EPILOGUE_DTYPES[epilogue]
    tn = min(MM_FULLK_TILE[1], N)
    tm = _fullk_rows(M, K, tn, dts)
    assert M % tm == 0 and N % tn == 0
    in_specs = [pl.BlockSpec((tm, K), lambda n, m: (m, 0))]
    out_specs = [pl.BlockSpec((tm, tn), lambda n, m: (m, n))] * len(dts)
    out_shape = [jax.ShapeDtypeStruct((M, N), d) for d in dts]
    if epilogue == "f32_bf16t":
        out_specs[1] = pl.BlockSpec((tn, tm), lambda n, m: (n, m))
        out_shape[1] = jax.ShapeDtypeStruct((N, M), dts[1])
    args = [a]
    if a2 is not None:
        epilogue2 = epilogue2 or epilogue
        M2 = a2.shape[0]
        dts2 = _EPILOGUE_DTYPES[epilogue2]
        in_specs.append(pl.BlockSpec((M2, K), lambda n, m: (0, 0)))
        out_specs += [pl.BlockSpec((M2, tn), lambda n, m: (0, n))] * len(dts2)
        out_shape += [jax.ShapeDtypeStruct((M2, N), d) for d in dts2]
        args.append(a2)
    in_specs.append(pl.BlockSpec((None, K, tn), lambda n, m: (layer, 0, n + col_block0)))
    outs = pl.pallas_call(
        functools.partial(_mm_fullk_kernel, epilogue=epilogue, epilogue2=epilogue2 if a2 is not None else None),
        grid=(N // tn, M // tm),
        in_specs=in_specs,
        out_specs=out_specs,
        out_shape=out_shape,
        scratch_shapes=[pltpu.VMEM((K, tn), BF16)],
        compiler_params=_params("arbitrary", "arbitrary"),
        name="mm_fullk_" + epilogue,
    )(*args, w)
    unpack = lambda o: o if len(o) > 1 else o[0]
    if a2 is None:
        return unpack(outs)
    return unpack(outs[:len(dts)]), unpack(outs[len(dts):])


def _mm_ktiled_kernel(*refs, with2):
    if with2:
        a_ref, a2_ref, w_ref, o_ref, o2_ref = refs
    else:
        a_ref, w_ref, o_ref = refs
    first_k = pl.program_id(2) == 0
    wb = w_ref[...].astype(BF16)
    part = jnp.dot(a_ref[...], wb, preferred_element_type=F32)

    @pl.when(first_k)
    def _():
        o_ref[...] = part

    @pl.when(jnp.logical_not(first_k))
    def _():
        o_ref[...] += part

    if with2:
        first_m = pl.program_id(0) == 0

        @pl.when(first_m & first_k)
        def _():
            o2_ref[...] = jnp.dot(a2_ref[...], wb, preferred_element_type=F32)

        @pl.when(first_m & jnp.logical_not(first_k))
        def _():
            o2_ref[...] += jnp.dot(a2_ref[...], wb, preferred_element_type=F32)


def _mm_ktiled(a, w, layer, a2=None):
    M, K = a.shape
    N = w.shape[2]
    tm = min(MM_KTILED_TILE[0], M)
    tn = min(MM_KTILED_TILE[1], N)
    tk = min(MM_KTILED_TILE[2], K)
    assert M % tm == 0 and N % tn == 0 and K % tk == 0
    in_specs = [pl.BlockSpec((tm, tk), lambda m, n, k: (m, k))]
    out_specs = [pl.BlockSpec((tm, tn), lambda m, n, k: (m, n))]
    out_shape = [jax.ShapeDtypeStruct((M, N), F32)]
    args = [a]
    if a2 is not None:
        M2 = a2.shape[0]
        last_n, last_k = N // tn - 1, K // tk - 1
        in_specs.append(pl.BlockSpec((M2, tk), lambda m, n, k: (0, jnp.where(m == 0, k, last_k))))
        out_specs.append(pl.BlockSpec((M2, tn), lambda m, n, k: (0, jnp.where(m == 0, n, last_n))))
        out_shape.append(jax.ShapeDtypeStruct((M2, N), F32))
        args.append(a2)
    in_specs.append(pl.BlockSpec((None, tk, tn), lambda m, n, k: (layer, k, n)))
    outs = pl.pallas_call(
        functools.partial(_mm_ktiled_kernel, with2=a2 is not None),
        grid=(M // tm, N // tn, K // tk),
        in_specs=in_specs,
        out_specs=out_specs,
        out_shape=out_shape,
        compiler_params=_params("arbitrary", "arbitrary", "arbitrary"),
        name="mm_ktiled",
    )(*args, w)
    return outs[0] if a2 is None else (outs[0], outs[1])


def _rglru_coeffs(conv, gw_ref, gb_ref, lam_ref):
    cb = conv.astype(BF16)
    gi = jnp.dot(cb, gw_ref[0, 0].astype(BF16), preferred_element_type=F32) + gb_ref[0:1, :]
    gr = jnp.dot(cb, gw_ref[1, 0].astype(BF16), preferred_element_type=F32) + gb_ref[1:2, :]
    i_gate = jax.nn.sigmoid(gi)
    r_gate = jax.nn.sigmoid(gr)
    log_a = LRU_C * r_gate * _log_sigmoid(lam_ref[...])
    a = jnp.exp(log_a)
    th = jnp.tanh(log_a)
    mult = jnp.sqrt(-2.0 * th / (1.0 - th))
    return a, mult * i_gate * conv


def _rglru_seq_kernel(x_ref, y_ref, h0_ref, c0_ref, cw_ref, cb_ref, gw_ref, gb_ref, lam_ref,
                      hg_ref, hl_ref, xs_ref, hc_ref, *, tT):
    t = pl.program_id(2)
    S = V7X_SUBLANES

    @pl.when(t == 0)
    def _():
        xs_ref[0:S, :] = c0_ref[0]
        hc_ref[...] = h0_ref[0]

    x = x_ref[0]
    xs_ref[S:S + tT, :] = x
    cw = cw_ref[...]
    nw = cw.shape[0]
    conv = cb_ref[...] + cw[nw - 1:nw, :] * x
    for j in range(1, nw):
        conv = conv + cw[nw - 1 - j:nw - j, :] * xs_ref[S - j:S - j + tT, :]
    xs_ref[0:S, :] = xs_ref[tT:tT + S, :]

    a, b = _rglru_coeffs(conv, gw_ref, gb_ref, lam_ref)

    row = lax.broadcasted_iota(jnp.int32, (S, a.shape[1]), 0)
    h = hc_ref[...]
    hs = []
    for r in range(0, tT, S):
        a8 = a[r:r + S]
        b8 = b[r:r + S]
        s = 1
        while s < S:
            keep = row >= s
            b8 = jnp.where(keep, a8 * pltpu.roll(b8, s, 0) + b8, b8)
            a8 = jnp.where(keep, a8 * pltpu.roll(a8, s, 0), a8)
            s *= 2
        h8 = a8 * h + b8
        hs.append(h8)
        h = h8[S - 1:S, :]
    hc_ref[...] = h
    hl_ref[0] = h
    hg_ref[0] = (jnp.concatenate(hs, axis=0) * _gelu_tanh(y_ref[0])).astype(hg_ref.dtype)


def _rglru_seq(xy, h0, conv0, conv_w, conv_b, gate_w, gate_b, lam):
    B, T, C2 = xy.shape
    C = C2 // 2
    nblk, cblk = gate_w.shape[1], gate_w.shape[2]
    nw = conv_w.shape[0]
    S = V7X_SUBLANES
    tT = min(512, T)
    assert T % tT == 0 and tT % S == 0 and nw - 1 <= S and cblk * nblk == C
    c0 = jnp.concatenate([jnp.zeros((B, S - (nw - 1), C), F32), conv0], axis=1)
    hg, hl = pl.pallas_call(
        functools.partial(_rglru_seq_kernel, tT=tT),
        grid=(B, nblk, T // tT),
        in_specs=[
            pl.BlockSpec((1, tT, cblk), lambda b, n, t: (b, t, n)),
            pl.BlockSpec((1, tT, cblk), lambda b, n, t: (b, t, n + nblk)),
            pl.BlockSpec((1, 1, cblk), lambda b, n, t: (b, 0, n)),
            pl.BlockSpec((1, S, cblk), lambda b, n, t: (b, 0, n)),
            pl.BlockSpec((nw, cblk), lambda b, n, t: (0, n)),
            pl.BlockSpec((1, cblk), lambda b, n, t: (0, n)),
            pl.BlockSpec((2, 1, cblk, cblk), lambda b, n, t: (0, n, 0, 0)),
            pl.BlockSpec((2, cblk), lambda b, n, t: (0, n)),
            pl.BlockSpec((1, cblk), lambda b, n, t: (0, n)),
        ],
        out_specs=[pl.BlockSpec((1, tT, cblk), lambda b, n, t: (b, t, n)),
                   pl.BlockSpec((1, 1, cblk), lambda b, n, t: (b, 0, n))],
        out_shape=[jax.ShapeDtypeStruct((B, T, C), BF16), jax.ShapeDtypeStruct((B, 1, C), F32)],
        scratch_shapes=[pltpu.VMEM((tT + S, cblk), F32), pltpu.VMEM((1, cblk), F32)],
        compiler_params=_params("parallel", "parallel", "arbitrary"),
        name="rglru_seq",
    )(xy, xy, h0.reshape(B, 1, C), c0, conv_w, conv_b.reshape(1, C), gate_w, gate_b, lam.reshape(1, C))
    return hg, hl.reshape(B, C)


def _rglru_step_kernel(x_ref, y_ref, h0_ref, c0_ref, cw_ref, cb_ref, gw_ref, gb_ref, lam_ref,
                       hg_ref, hl_ref):
    x = x_ref[...]
    cw = cw_ref[...]
    nw = cw.shape[0]
    conv = cb_ref[...] + cw[nw - 1:nw, :] * x
    for j in range(1, nw):
        conv = conv + cw[nw - 1 - j:nw - j, :] * c0_ref[nw - 1 - j]
    a, b = _rglru_coeffs(conv, gw_ref, gb_ref, lam_ref)
    h = a * h0_ref[...] + b
    hl_ref[...] = h
    hg_ref[...] = (h * _gelu_tanh(y_ref[...])).astype(hg_ref.dtype)


def _rglru_step(xy, h0, conv0, conv_w, conv_b, gate_w, gate_b, lam):
    B, C2 = xy.shape
    C = C2 // 2
    nblk, cblk = gate_w.shape[1], gate_w.shape[2]
    nw = conv_w.shape[0]
    c0 = jnp.swapaxes(conv0, 0, 1)
    return pl.pallas_call(
        _rglru_step_kernel,
        grid=(nblk,),
        in_specs=[
            pl.BlockSpec((B, cblk), lambda n: (0, n)),
            pl.BlockSpec((B, cblk), lambda n: (0, n + nblk)),
            pl.BlockSpec((B, cblk), lambda n: (0, n)),
            pl.BlockSpec((nw - 1, B, cblk), lambda n: (0, 0, n)),
            pl.BlockSpec((nw, cblk), lambda n: (0, n)),
            pl.BlockSpec((1, cblk), lambda n: (0, n)),
            pl.BlockSpec((2, 1, cblk, cblk), lambda n: (0, n, 0, 0)),
            pl.BlockSpec((2, cblk), lambda n: (0, n)),
            pl.BlockSpec((1, cblk), lambda n: (0, n)),
        ],
        out_specs=[pl.BlockSpec((B, cblk), lambda n: (0, n)), pl.BlockSpec((B, cblk), lambda n: (0, n))],
        out_shape=[jax.ShapeDtypeStruct((B, C), BF16), jax.ShapeDtypeStruct((B, C), F32)],
        compiler_params=_params("parallel"),
        name="rglru_step",
    )(xy, xy, h0, c0, conv_w, conv_b.reshape(1, C), gate_w, gate_b, lam.reshape(1, C))


def _tri_dots(lp, carry_col_ref, carry_row_ref, cum_col_ref, cum_row_ref):
    P = lp.shape[0]
    hi = lax.Precision.HIGHEST
    r = lax.broadcasted_iota(jnp.int32, (P, P), 0)
    c = lax.broadcasted_iota(jnp.int32, (P, P), 1)
    ones = jnp.ones((P, P), F32)
    if cum_col_ref is not None:
        tril = (c <= r).astype(F32)
        cum_col_ref[0] = jnp.dot(tril, lp, precision=hi, preferred_element_type=F32) + carry_col_ref[...]
        carry_col_ref[...] += jnp.dot(ones, lp, precision=hi, preferred_element_type=F32)
    lpt = lp.T
    triu = (r <= c).astype(F32)
    cum_row_ref[0] = jnp.dot(lpt, triu, precision=hi, preferred_element_type=F32) + carry_row_ref[...]
    carry_row_ref[...] += jnp.dot(lpt, ones, precision=hi, preferred_element_type=F32)


def _gate_logits(f, n_heads):
    return jnp.where(lax.broadcasted_iota(jnp.int32, f.shape, f.ndim - 1) < n_heads, f, 0.0)


def _logf_cum_kernel(f_ref, bf_ref, lf_ref, cc_ref, cr_ref, car_c, car_r, *, n_heads):
    @pl.when(pl.program_id(1) == 0)
    def _():
        car_c[...] = jnp.zeros_like(car_c)
        car_r[...] = jnp.zeros_like(car_r)

    lp = _log_sigmoid(_gate_logits(f_ref[0], n_heads) + bf_ref[...])
    lf_ref[0] = lp
    _tri_dots(lp, car_c, car_r, cc_ref, cr_ref)


def _logf_cum(f, b_f_pad, n_heads):
    B, T, L = f.shape
    P = V7X_LANES
    assert T % P == 0 and L == P
    return pl.pallas_call(
        functools.partial(_logf_cum_kernel, n_heads=n_heads),
        grid=(B, T // P),
        in_specs=[pl.BlockSpec((1, P, L), lambda b, t: (b, t, 0)), pl.BlockSpec((1, L), lambda b, t: (0, 0))],
        out_specs=[pl.BlockSpec((1, P, L), lambda b, t: (b, t, 0)),
                   pl.BlockSpec((1, P, L), lambda b, t: (b, t, 0)),
                   pl.BlockSpec((1, L, P), lambda b, t: (b, 0, t))],
        out_shape=[jax.ShapeDtypeStruct((B, T, L), F32), jax.ShapeDtypeStruct((B, T, L), F32),
                   jax.ShapeDtypeStruct((B, L, T), F32)],
        scratch_shapes=[pltpu.VMEM((P, L), F32), pltpu.VMEM((L, P), F32)],
        compiler_params=_params("parallel", "arbitrary"),
        name="logf_cum",
    )(f, b_f_pad)


def _paged_cum_kernel(pt_ref, *refs, n_in):
    lf_refs = refs[:n_in]
    cc_ref, tot_ref, car = refs[n_in:]

    @pl.when(pl.program_id(1) == 0)
    def _():
        car[...] = jnp.zeros_like(car)

    P = lf_refs[0].shape[1]
    hi = lax.Precision.HIGHEST
    tril = (lax.broadcasted_iota(jnp.int32, (P, P), 1) <= lax.broadcasted_iota(jnp.int32, (P, P), 0)).astype(F32)
    ones = jnp.ones((P, P), F32)
    for g, lf_ref in enumerate(lf_refs):
        lp = lf_ref[0]
        cc_ref[0, g * P:(g + 1) * P, :] = jnp.dot(tril, lp, precision=hi, preferred_element_type=F32) + car[...]
        car[...] += jnp.dot(ones, lp, precision=hi, preferred_element_type=F32)
    tot_ref[0] = car[0:V7X_SUBLANES, :]


def _paged_cum(cache_logf, page_table):
    n_pool, P, H = cache_logf.shape
    DB, n_pages = page_table.shape
    G = math.gcd(n_pages, 8)
    S = V7X_SUBLANES
    specs = [pl.BlockSpec((1, P, H), lambda b, s, pt, g=g: (pt[b, s * G + g], 0, 0)) for g in range(G)]
    cum, tot = pl.pallas_call(
        functools.partial(_paged_cum_kernel, n_in=G),
        grid_spec=pltpu.PrefetchScalarGridSpec(
            num_scalar_prefetch=1,
            grid=(DB, n_pages // G),
            in_specs=specs,
            out_specs=[pl.BlockSpec((1, G * P, H), lambda b, s, pt: (b, s, 0)),
                       pl.BlockSpec((1, S, H), lambda b, s, pt: (b, 0, 0))],
            scratch_shapes=[pltpu.VMEM((P, H), F32)],
        ),
        out_shape=[jax.ShapeDtypeStruct((DB, n_pages * P, H), F32), jax.ShapeDtypeStruct((DB, S, H), F32)],
        compiler_params=_params("parallel", "arbitrary"),
        name="paged_cum",
    )(page_table, *([cache_logf] * G))
    return cum, tot[:, 0, :]


FOX_HEADS_PER_STEP = 4


def _fox_prompt_kernel(q_ref, k_ref, vt_ref, cq_ref, ck_ref, o_ref, *, tq, hd, hb, nq, scale):
    qi = pl.program_id(2)
    nt = (((1,), (1,)), ((), ()))
    below_or_on = (lax.broadcasted_iota(jnp.int32, (tq, 1), 0)
                   <= lax.broadcasted_iota(jnp.int32, (1, tq), 1))

    def q_block(c):
        lo = c * tq
        heads = [slice(j * hd, (j + 1) * hd) for j in range(hb)]
        scale2 = scale * LOG2E
        scores = []
        for j, cols in enumerate(heads):
            qj = q_ref[0, :, cols]
            cqj = cq_ref[0, 0, j:j + 1, :] * LOG2E
            sd = lax.dot_general(k_ref[0, lo:lo + tq, cols], qj, nt, preferred_element_type=F32) * scale2
            sd = sd + (cqj - ck_ref[0, 0, lo:lo + tq, j:j + 1] * LOG2E)
            sd = jnp.where(below_or_on, sd, -jnp.inf)
            m = jnp.max(sd, axis=0, keepdims=True)
            sp = None
            if c:
                sp = lax.dot_general(k_ref[0, 0:lo, cols], qj, nt, preferred_element_type=F32) * scale2
                sp = sp + (cqj - ck_ref[0, 0, 0:lo, j:j + 1] * LOG2E)
                m = jnp.maximum(m, jnp.max(sp, axis=0, keepdims=True))
            scores.append((sd, sp, m))
        probs = []
        for sd, sp, m in scores:
            pd = jnp.exp2(sd - m)
            l = jnp.sum(pd, axis=0, keepdims=True)
            pp = None
            if c:
                pp = jnp.exp2(sp - m)
                l = l + jnp.sum(pp, axis=0, keepdims=True)
                pp = pp.astype(BF16)
            probs.append((pd.astype(BF16), pp, l))
        for cols, (pd, pp, l) in zip(heads, probs):
            acc = jnp.dot(vt_ref[cols, lo:lo + tq], pd, preferred_element_type=F32)
            if c:
                acc = acc + jnp.dot(vt_ref[cols, 0:lo], pp, preferred_element_type=F32)
            o_ref[0, :, cols] = (acc / l).T.astype(o_ref.dtype)

    for c in range(nq):
        pl.when(qi == c)(functools.partial(q_block, c))


def _fox_prompt(q, k, vt, cum_col, cum_row, n_heads):
    B, T, D = q.shape
    hd = D // n_heads
    hb = min(FOX_HEADS_PER_STEP, n_heads)
    hg = n_heads // hb
    tq = min(256, T)
    assert n_heads % hb == 0 and T % tq == 0
    cq = cum_row[:, :n_heads, :].reshape(B, hg, hb, T)
    ck = jnp.transpose(cum_col[:, :, :n_heads].reshape(B, T, hg, hb), (0, 2, 1, 3))
    return pl.pallas_call(
        functools.partial(_fox_prompt_kernel, tq=tq, hd=hd, hb=hb, nq=T // tq, scale=hd ** -0.5),
        grid=(B, hg, T // tq),
        in_specs=[pl.BlockSpec((1, tq, hb * hd), lambda b, g, i: (b, i, g)),
                  pl.BlockSpec((1, T, hb * hd), lambda b, g, i: (b, 0, g)),
                  pl.BlockSpec((hb * hd, T), lambda b, g, i: (g, b)),
                  pl.BlockSpec((1, 1, hb, tq), lambda b, g, i: (b, g, 0, i)),
                  pl.BlockSpec((1, 1, T, hb), lambda b, g, i: (b, g, 0, 0))],
        out_specs=pl.BlockSpec((1, tq, hb * hd), lambda b, g, i: (b, i, g)),
        out_shape=jax.ShapeDtypeStruct((B, T, D), BF16),
        compiler_params=_params("parallel", "parallel", "arbitrary"),
        name="fox_prompt",
    )(q, k, vt, cq, ck)


DECODE_PAGES_PER_STEP = 4


def _fox_decode_kernel(pt_ref, q_ref, *refs, n_pg, scale):
    k_refs, v_refs = refs[:n_pg], refs[n_pg:2 * n_pg]
    ckf_ref, cqf_ref, cqb_ref, kn_ref, vn_ref, o_ref, m_ref, l_ref, acc_ref = refs[2 * n_pg:]
    p_idx = pl.program_id(1)
    H = q_ref.shape[1]

    @pl.when(p_idx == 0)
    def _():
        m_ref[...] = jnp.full_like(m_ref, -jnp.inf)
        l_ref[...] = jnp.zeros_like(l_ref)
        acc_ref[...] = jnp.zeros_like(acc_ref)

    qb = q_ref[0].astype(BF16)
    rows = k_refs[0].shape[1]
    own = (lax.broadcasted_iota(jnp.int32, (H, rows), 1) % H) == lax.broadcasted_iota(jnp.int32, (H, rows), 0)
    m, l, acc = m_ref[...], l_ref[...], acc_ref[...]
    for g in range(n_pg):
        kb = k_refs[g][0].astype(BF16)
        s = lax.dot_general(qb, kb, (((1,), (1,)), ((), ())), preferred_element_type=F32) * scale
        s = s + (cqf_ref[0] - ckf_ref[0, 0, g:g + 1, :])
        s = jnp.where(own, s, -jnp.inf)
        m_new = jnp.maximum(m, jnp.max(s, axis=1, keepdims=True))
        alpha = jnp.exp(m - m_new)
        p = jnp.exp(s - m_new[:, 0:1])
        l = alpha * l + jnp.sum(p, axis=1, keepdims=True)
        acc = alpha * acc + jnp.dot(p.astype(BF16), v_refs[g][0].astype(BF16), preferred_element_type=F32)
        m = m_new
    m_ref[...] = m
    l_ref[...] = l
    acc_ref[...] = acc

    @pl.when(p_idx == pl.num_programs(1) - 1)
    def _():
        qn = q_ref[0].astype(BF16).astype(F32)
        kn = kn_ref[0].astype(BF16).astype(F32)
        vn = vn_ref[0].astype(BF16).astype(F32)
        cq = cqb_ref[0]
        s_n = jnp.sum(qn * kn, axis=1, keepdims=True) * scale + (cq - cq)
        m_p = m_ref[...]
        m_f = jnp.maximum(m_p, s_n)
        al = jnp.exp(m_p - m_f)
        p_n = jnp.exp(s_n - m_f)
        l_f = al * l_ref[...] + p_n
        o = al * acc_ref[...] + p_n.astype(BF16).astype(F32) * vn
        o_ref[0] = (o / l_f).astype(o_ref.dtype)


def _fox_decode(q, k_new, v_new, cache_k, cache_v, page_table, cum_past, cum_new):
    DB, H, hd = q.shape
    n_pool, P = cache_k.shape[0], cache_k.shape[1]
    n_pages = page_table.shape[1]
    R = P * H
    assert hd == V7X_LANES
    k2 = cache_k.reshape(n_pool, R, hd)
    v2 = cache_v.reshape(n_pool, R, hd)
    G = math.gcd(n_pages, DECODE_PAGES_PER_STEP)
    ckf = cum_past.reshape(DB, n_pages // G, G, R)
    cqf = jnp.tile(cum_new, (1, P)).reshape(DB, 1, R)
    cqb = jnp.broadcast_to(cum_new[:, :, None], (DB, H, hd))
    per_seq = lambda shape: pl.BlockSpec((1,) + shape, lambda b, s, pt: (b, 0, 0))
    page = [pl.BlockSpec((1, R, hd), lambda b, s, pt, g=g: (pt[b, s * G + g], 0, 0)) for g in range(G)]
    return pl.pallas_call(
        functools.partial(_fox_decode_kernel, n_pg=G, scale=hd ** -0.5),
        grid_spec=pltpu.PrefetchScalarGridSpec(
            num_scalar_prefetch=1,
            grid=(DB, n_pages // G),
            in_specs=[per_seq((H, hd))] + page + page
                     + [pl.BlockSpec((1, 1, G, R), lambda b, s, pt: (b, s, 0, 0)),
                        per_seq((1, R)), per_seq((H, hd)), per_seq((H, hd)), per_seq((H, hd))],
            out_specs=per_seq((H, hd)),
            scratch_shapes=[pltpu.VMEM((H, hd), F32), pltpu.VMEM((H, hd), F32), pltpu.VMEM((H, hd), F32)],
        ),
        out_shape=jax.ShapeDtypeStruct((DB, H, hd), BF16),
        compiler_params=_params("parallel", "arbitrary"),
        name="fox_decode",
    )(page_table, q, *([k2] * G), *([v2] * G), ckf, cqf, cqb, k_new, v_new)


def _forward(x_prompt, x_sample, h0_s, conv0_s, cache_k, cache_v, cache_logf, page_table, p):
    B, T, D = x_prompt.shape
    DB, DT, _ = x_sample.shape
    assert DT == 1, "the sample group carries one new token per sequence"
    M = B * T
    depth = p["w_up"].shape[0]
    n_a = p["w_in"].shape[0]
    H = p["b_f"].shape[0]
    hd = p["w_q"].shape[2] // H
    d_attn = H * hd
    nw = p["conv_w"].shape[1]
    assert T >= nw - 1

    xp = x_prompt.reshape(M, D)
    xs = x_sample.reshape(DB, D)
    xnp = _norm_cast(xp, p["norm_mix_pre"][0])
    xns = _norm_cast(xs, p["norm_mix_pre"][0])
    hs_p, convs_p, hs_s, convs_s = [], [], [], []
    kv = None
    for l in range(depth):
        if l < n_a:
            xy_p, xy_s = _mm_fullk(xnp, p["w_in"], l, a2=xns, epilogue="f32")
            C = xy_p.shape[1] // 2
            args = (p["conv_w"][l], p["conv_b"][l], p["gate_w"][l], p["gate_b"][l], p["rg_lambda"][l])
            xy_p3 = xy_p.reshape(B, T, 2 * C)
            hg_p, h_p = _rglru_seq(xy_p3, jnp.zeros((B, C), F32), jnp.zeros((B, nw - 1, C), F32), *args)
            hg_s, h_s = _rglru_step(xy_s, h0_s[l], conv0_s[l], *args)
            hs_p.append(h_p)
            hs_s.append(h_s)
            convs_p.append(xy_p3[:, T - (nw - 1):, :C])
            convs_s.append(jnp.concatenate([conv0_s[l], xy_s[:, None, :C]], axis=1)[:, 1:])
            m_p, m_s = _mm_fullk(hg_p.reshape(M, C), p["w_out_a"], l, a2=hg_s, epilogue="f32")
        else:
            j = l - n_a
            q_p, q_s = _mm_fullk(xnp, p["w_q"], j, a2=xns, epilogue="bf16", epilogue2="f32")
            o_p = _fox_prompt(q_p.reshape(B, T, d_attn), kv["kb"].reshape(B, T, d_attn), kv["vt"],
                              kv["cum_col"], kv["cum_row"], H)
            o_s = _fox_decode(q_s.reshape(DB, H, hd), kv["k_s"].reshape(DB, H, hd), kv["v_s"].reshape(DB, H, hd),
                              cache_k, cache_v, page_table, kv["cum_past"], kv["cum_new"])
            m_p, m_s = _mm_fullk(o_p.reshape(M, d_attn), p["w_o"], j, a2=o_s.reshape(DB, d_attn), epilogue="f32")
        xp, (xn2p,) = _resid_norm(xp, m_p, p["norm_mix_post"][l], [p["norm_mlp_pre"][l]])
        xs, (xn2s,) = _resid_norm(xs, m_s, p["norm_mix_post"][l], [p["norm_mlp_pre"][l]])
        hmid_p, hmid_s = _mm_fullk(xn2p, p["w_up"], l, a2=xn2s, epilogue="relu2_bf16")
        f_p, f_s = _mm_ktiled(hmid_p, p["w_down"], l, a2=hmid_s)
        if l + 1 == n_a:
            gains = [p["norm_kv"], p["norm_mix_pre"][l + 1]]
            xp, (xkv_p, xnp) = _resid_norm(xp, f_p, p["norm_mlp_post"][l], gains)
            xs, (xkv_s, xns) = _resid_norm(xs, f_s, p["norm_mlp_post"][l], gains)
            kv = _shared_kv(xkv_p, xkv_s, B, T, H, hd, cache_logf, page_table, p)
        elif l + 1 < depth:
            xp, (xnp,) = _resid_norm(xp, f_p, p["norm_mlp_post"][l], [p["norm_mix_pre"][l + 1]])
            xs, (xns,) = _resid_norm(xs, f_s, p["norm_mlp_post"][l], [p["norm_mix_pre"][l + 1]])
        else:
            xp, _ = _resid_norm(xp, f_p, p["norm_mlp_post"][l], [])
            xs, _ = _resid_norm(xs, f_s, p["norm_mlp_post"][l], [])
    return (xp.reshape(B, T, D), xs.reshape(DB, DT, D),
            jnp.stack(hs_p), jnp.stack(convs_p),
            kv["k_p"].reshape(B, T, H, hd), kv["v_p"].reshape(B, T, H, hd), kv["logf_p"],
            jnp.stack(hs_s), jnp.stack(convs_s),
            kv["k_s"].reshape(DB, DT, H, hd), kv["v_s"].reshape(DB, DT, H, hd), kv["logf_s"])


def _shared_kv(xkv_p, xkv_s, B, T, H, hd, cache_logf, page_table, p):
    d_attn = H * hd
    DB = xkv_s.shape[0]
    L = V7X_LANES
    w_kvf = p["w_kvf"]
    tn = min(512, d_attn)
    (k_p, kb), k_s = _mm_fullk(xkv_p, w_kvf, a2=xkv_s, n_cols=d_attn, col_block0=0,
                               epilogue="f32_bf16", epilogue2="f32")
    (v_p, vt), v_s = _mm_fullk(xkv_p, w_kvf, a2=xkv_s, n_cols=d_attn, col_block0=d_attn // tn,
                               epilogue="f32_bf16t", epilogue2="f32")
    assert (2 * d_attn) % L == 0 and H <= L
    b_f = jnp.pad(p["b_f"], (0, L - H)).reshape(1, L)
    f_p, f_s = _mm_fullk(xkv_p, w_kvf, a2=xkv_s, n_cols=L, col_block0=2 * d_attn // L, epilogue="f32")
    logf_p, cum_col, cum_row = _logf_cum(f_p.reshape(B, T, L), b_f, H)
    cum_past, tot = _paged_cum(cache_logf, page_table)
    logf_s, cum_new = _logf_new(f_s, b_f, tot)
    return dict(k_p=k_p, v_p=v_p, kb=kb, vt=vt, logf_p=logf_p[:, :, :H], cum_col=cum_col, cum_row=cum_row,
                k_s=k_s, v_s=v_s, logf_s=logf_s[:, :H].reshape(DB, 1, H), cum_past=cum_past, cum_new=cum_new)


def _logf_new_kernel(f_ref, bf_ref, tot_ref, lf_ref, cn_ref):
    H = tot_ref.shape[1]
    lp = _log_sigmoid(_gate_logits(f_ref[...], H) + bf_ref[...])
    lf_ref[...] = lp
    cn_ref[...] = tot_ref[...] + lp[:, :H]


def _logf_new(f, b_f_pad, tot):
    DB, L = f.shape
    H = tot.shape[1]
    return pl.pallas_call(
        _logf_new_kernel,
        out_shape=[jax.ShapeDtypeStruct((DB, L), F32), jax.ShapeDtypeStruct((DB, H), F32)],
        compiler_params=pltpu.CompilerParams(vmem_limit_bytes=VMEM_LIMIT),
        name="logf_new",
    )(f, b_f_pad, tot)


def kernel(x_prompt, x_sample, state_rglru_h, state_conv, cache_k, cache_v, cache_logf, page_table,
           w_in, conv_w, conv_b, gate_w, gate_b, rg_lambda, w_out_a, norm_kv, w_kvf, b_f, w_q, w_o,
           norm_mix_pre, norm_mix_post, norm_mlp_pre, norm_mlp_post, w_up, w_down):
    p = dict(w_in=w_in, conv_w=conv_w, conv_b=conv_b, gate_w=gate_w, gate_b=gate_b, rg_lambda=rg_lambda,
             w_out_a=w_out_a, norm_kv=norm_kv, w_kvf=w_kvf, b_f=b_f, w_q=w_q, w_o=w_o,
             norm_mix_pre=norm_mix_pre, norm_mix_post=norm_mix_post, norm_mlp_pre=norm_mlp_pre,
             norm_mlp_post=norm_mlp_post, w_up=w_up, w_down=w_down)
    return _forward(x_prompt, x_sample, state_rglru_h, state_conv, cache_k, cache_v, cache_logf, page_table, p)
```

```python
import functools
import math

import jax
import jax.numpy as jnp
from jax import lax
from jax.experimental import pallas as pl
from jax.experimental.pallas import tpu as pltpu

F32 = jnp.float32
BF16 = jnp.bfloat16

RMS_EPS = 1e-6
LRU_C = 8.0
LOG2E = math.log2(math.e)

V7X_VMEM_BYTES = 64 * 1024 * 1024
V7X_LANES = 128
V7X_SUBLANES = 8
VMEM_LIMIT = V7X_VMEM_BYTES - 2 * 1024 * 1024
MM_SPILL_ALLOWANCE = 5 * 1024 * 1024
NORM_ROWS = 256


def _params(*sem):
    return pltpu.CompilerParams(dimension_semantics=sem, vmem_limit_bytes=VMEM_LIMIT)


def _rms_scale(x):
    return x * lax.rsqrt(jnp.mean(x * x, axis=-1, keepdims=True) + RMS_EPS)


def _log_sigmoid(x):
    return jnp.minimum(x, 0.0) - jnp.log1p(jnp.exp(-jnp.abs(x)))


def _gelu_tanh(x):
    c = math.sqrt(2.0 / math.pi)
    return 0.5 * x * (1.0 + jnp.tanh(c * (x + 0.044715 * (x * x * x))))


def _norm_cast_kernel(x_ref, g_ref, o_ref):
    o_ref[...] = (_rms_scale(x_ref[...]) * g_ref[...]).astype(o_ref.dtype)


def _norm_cast(x, g):
    M, D = x.shape
    tr = min(NORM_ROWS, M)
    return pl.pallas_call(
        _norm_cast_kernel,
        grid=(M // tr,),
        in_specs=[pl.BlockSpec((tr, D), lambda i: (i, 0)), pl.BlockSpec((1, D), lambda i: (0, 0))],
        out_specs=pl.BlockSpec((tr, D), lambda i: (i, 0)),
        out_shape=jax.ShapeDtypeStruct((M, D), BF16),
        compiler_params=_params("parallel"),
        name="norm_cast",
    )(x, g.reshape(1, D))


def _resid_norm_kernel(x_ref, m_ref, gp_ref, gn_ref, xo_ref, *n_refs):
    xn = x_ref[...] + _rms_scale(m_ref[...]) * gp_ref[...]
    xo_ref[...] = xn
    if n_refs:
        y = _rms_scale(xn)
        for j, r in enumerate(n_refs):
            r[...] = (y * gn_ref[j:j + 1, :]).astype(r.dtype)


def _resid_norm(x, m, g_post, g_next):
    M, D = x.shape
    n = len(g_next)
    tr = min(NORM_ROWS, M)
    gn = jnp.stack(g_next) if n else jnp.zeros((1, D), F32)
    row = pl.BlockSpec((tr, D), lambda i: (i, 0))
    outs = pl.pallas_call(
        _resid_norm_kernel,
        grid=(M // tr,),
        in_specs=[row, row, pl.BlockSpec((1, D), lambda i: (0, 0)),
                  pl.BlockSpec((gn.shape[0], D), lambda i: (0, 0))],
        out_specs=[row] * (1 + n),
        out_shape=[jax.ShapeDtypeStruct((M, D), F32)] + [jax.ShapeDtypeStruct((M, D), BF16)] * n,
        compiler_params=_params("parallel"),
        name="resid_norm",
    )(x, m, g_post.reshape(1, D), gn)
    return outs[0], list(outs[1:])


def _mm_store(acc, o_refs, epilogue):
    if epilogue == "f32":
        o_refs[0][...] = acc
    elif epilogue == "bf16":
        o_refs[0][...] = acc.astype(BF16)
    elif epilogue == "f32_bf16":
        o_refs[0][...] = acc
        o_refs[1][...] = acc.astype(BF16)
    elif epilogue == "f32_bf16t":
        o_refs[0][...] = acc
        o_refs[1][...] = acc.T.astype(BF16)
    elif epilogue == "relu2_bf16":
        h = jnp.maximum(acc, 0.0)
        o_refs[0][...] = (h * h).astype(BF16)
    else:
        raise ValueError(epilogue)


MM_FULLK_TILE = (2048, 512)
MM_KTILED_TILE = (2048, 1024, 2048)
MM_KTILED_OUT_BUFFERS = 1


def _fullk_rows(M, K, tn, out_dtypes):
    tm = min(MM_FULLK_TILE[0], M)
    while True:
        need = (2 * tm * K * 2 + 2 * K * tn * 4 + K * tn * 2
                + sum(2 * tm * tn * jnp.dtype(d).itemsize for d in out_dtypes) + MM_SPILL_ALLOWANCE)
        if need <= VMEM_LIMIT or tm % 2 or tm <= V7X_SUBLANES:
            return tm
        tm //= 2

_EPILOGUE_DTYPES = {"f32": (F32,), "bf16": (BF16,), "f32_bf16": (F32, BF16), "f32_bf16t": (F32, BF16),
                    "relu2_bf16": (BF16,)}


def _mm_fullk_kernel(*refs, epilogue, epilogue2):
    n1 = len(_EPILOGUE_DTYPES[epilogue])
    if epilogue2 is None:
        a_ref, w_ref, *o_refs, wb_ref = refs
    else:
        a_ref, a2_ref, w_ref, *o_refs, wb_ref = refs

    @pl.when(pl.program_id(1) == 0)
    def _():
        wb_ref[...] = w_ref[...].astype(BF16)
        if epilogue2 is not None:
            acc2 = jnp.dot(a2_ref[...], wb_ref[...], preferred_element_type=F32)
            _mm_store(acc2, o_refs[n1:], epilogue2)

    acc = jnp.dot(a_ref[...], wb_ref[...], preferred_element_type=F32)
    _mm_store(acc, o_refs[:n1], epilogue)


def _mm_fullk(a, w, layer=0, *, a2=None, n_cols=None, col_block0=0, epilogue="f32", epilogue2=None):
    if w.ndim == 2:
        w = w[None]
    M, K = a.shape
    N = w.shape[2] if n_cols is None else n_cols
    dts = _EPILOGUE_DTYPES[epilogue]
    tn = min(MM_FULLK_TILE[1], N)
    tm = _fullk_rows(M, K, tn, dts)
    assert M % tm == 0 and N % tn == 0
    in_specs = [pl.BlockSpec((tm, K), lambda n, m: (m, 0))]
    out_specs = [pl.BlockSpec((tm, tn), lambda n, m: (m, n))] * len(dts)
    out_shape = [jax.ShapeDtypeStruct((M, N), d) for d in dts]
    if epilogue == "f32_bf16t":
        out_specs[1] = pl.BlockSpec((tn, tm), lambda n, m: (n, m))
        out_shape[1] = jax.ShapeDtypeStruct((N, M), dts[1])
    args = [a]
    if a2 is not None:
        epilogue2 = epilogue2 or epilogue
        M2 = a2.shape[0]
        dts2 = _EPILOGUE_DTYPES[epilogue2]
        in_specs.append(pl.BlockSpec((M2, K), lambda n, m: (0, 0)))
        out_specs += [pl.BlockSpec((M2, tn), lambda n, m: (0, n))] * len(dts2)
        out_shape += [jax.ShapeDtypeStruct((M2, N), d) for d in dts2]
        args.append(a2)
    in_specs.append(pl.BlockSpec((None, K, tn), lambda n, m: (layer, 0, n + col_block0)))
    outs = pl.pallas_call(
        functools.partial(_mm_fullk_kernel, epilogue=epilogue, epilogue2=epilogue2 if a2 is not None else None),
        grid=(N // tn, M // tm),
        in_specs=in_specs,
        out_specs=out_specs,
        out_shape=out_shape,
        scratch_shapes=[pltpu.VMEM((K, tn), BF16)],
        compiler_params=_params("arbitrary", "arbitrary"),
        name="mm_fullk_" + epilogue,
    )(*args, w)
    unpack = lambda o: o if len(o) > 1 else o[0]
    if a2 is None:
        return unpack(outs)
    return unpack(outs[:len(dts)]), unpack(outs[len(dts):])


def _mm_ktiled_kernel(*refs, with2):
    if with2:
        a_ref, a2_ref, w_ref, o_ref, o2_ref = refs
    else:
        a_ref, w_ref, o_ref = refs
    first_k = pl.program_id(2) == 0
    wb = w_ref[...].astype(BF16)
    part = jnp.dot(a_ref[...], wb, preferred_element_type=F32)

    @pl.when(first_k)
    def _():
        o_ref[...] = part

    @pl.when(jnp.logical_not(first_k))
    def _():
        o_ref[...] += part

    if with2:
        first_m = pl.program_id(0) == 0

        @pl.when(first_m & first_k)
        def _():
            o2_ref[...] = jnp.dot(a2_ref[...], wb, preferred_element_type=F32)

        @pl.when(first_m & jnp.logical_not(first_k))
        def _():
            o2_ref[...] += jnp.dot(a2_ref[...], wb, preferred_element_type=F32)


def _mm_ktiled(a, w, layer, a2=None):
    M, K = a.shape
    N = w.shape[2]
    tm = min(MM_KTILED_TILE[0], M)
    tn = min(MM_KTILED_TILE[1], N)
    tk = min(MM_KTILED_TILE[2], K)
    assert M % tm == 0 and N % tn == 0 and K % tk == 0
    in_specs = [pl.BlockSpec((tm, tk), lambda m, n, k: (m, k))]
    out_specs = [pl.BlockSpec((tm, tn), lambda m, n, k: (m, n), pipeline_mode=pl.Buffered(MM_KTILED_OUT_BUFFERS))]
    out_shape = [jax.ShapeDtypeStruct((M, N), F32)]
    args = [a]
    if a2 is not None:
        M2 = a2.shape[0]
        last_n, last_k = N // tn - 1, K // tk - 1
        in_specs.append(pl.BlockSpec((M2, tk), lambda m, n, k: (0, jnp.where(m == 0, k, last_k))))
        out_specs.append(pl.BlockSpec((M2, tn), lambda m, n, k: (0, jnp.where(m == 0, n, last_n))))
        out_shape.append(jax.ShapeDtypeStruct((M2, N), F32))
        args.append(a2)
    in_specs.append(pl.BlockSpec((None, tk, tn), lambda m, n, k: (layer, k, n)))
    outs = pl.pallas_call(
        functools.partial(_mm_ktiled_kernel, with2=a2 is not None),
        grid=(M // tm, N // tn, K // tk),
        in_specs=in_specs,
        out_specs=out_specs,
        out_shape=out_shape,
        compiler_params=_params("arbitrary", "arbitrary", "arbitrary"),
        name="mm_ktiled",
    )(*args, w)
    return outs[0] if a2 is None else (outs[0], outs[1])


def _rglru_coeffs(conv, gw_ref, gb_ref, lam_ref):
    cb = conv.astype(BF16)
    gi = jnp.dot(cb, gw_ref[0, 0].astype(BF16), preferred_element_type=F32) + gb_ref[0:1, :]
    gr = jnp.dot(cb, gw_ref[1, 0].astype(BF16), preferred_element_type=F32) + gb_ref[1:2, :]
    i_gate = jax.nn.sigmoid(gi)
    r_gate = jax.nn.sigmoid(gr)
    log_a = LRU_C * r_gate * _log_sigmoid(lam_ref[...])
    a = jnp.exp(log_a)
    th = jnp.tanh(log_a)
    mult = jnp.sqrt(-2.0 * th / (1.0 - th))
    return a, mult * i_gate * conv


def _rglru_seq_kernel(x_ref, y_ref, h0_ref, c0_ref, cw_ref, cb_ref, gw_ref, gb_ref, lam_ref,
                      hg_ref, hl_ref, xs_ref, hc_ref, *, tT):
    t = pl.program_id(2)
    S = V7X_SUBLANES

    @pl.when(t == 0)
    def _():
        xs_ref[0:S, :] = c0_ref[0]
        hc_ref[...] = h0_ref[0]

    x = x_ref[0]
    xs_ref[S:S + tT, :] = x
    cw = cw_ref[...]
    nw = cw.shape[0]
    conv = cb_ref[...] + cw[nw - 1:nw, :] * x
    for j in range(1, nw):
        conv = conv + cw[nw - 1 - j:nw - j, :] * xs_ref[S - j:S - j + tT, :]
    xs_ref[0:S, :] = xs_ref[tT:tT + S, :]

    a, b = _rglru_coeffs(conv, gw_ref, gb_ref, lam_ref)

    row = lax.broadcasted_iota(jnp.int32, (S, a.shape[1]), 0)
    h = hc_ref[...]
    hs = []
    for r in range(0, tT, S):
        a8 = a[r:r + S]
        b8 = b[r:r + S]
        s = 1
        while s < S:
            keep = row >= s
            b8 = jnp.where(keep, a8 * pltpu.roll(b8, s, 0) + b8, b8)
            a8 = jnp.where(keep, a8 * pltpu.roll(a8, s, 0), a8)
            s *= 2
        h8 = a8 * h + b8
        hs.append(h8)
        h = h8[S - 1:S, :]
    hc_ref[...] = h
    hl_ref[0] = h
    hg_ref[0] = (jnp.concatenate(hs, axis=0) * _gelu_tanh(y_ref[0])).astype(hg_ref.dtype)


def _rglru_seq(xy, h0, conv0, conv_w, conv_b, gate_w, gate_b, lam):
    B, T, C2 = xy.shape
    C = C2 // 2
    nblk, cblk = gate_w.shape[1], gate_w.shape[2]
    nw = conv_w.shape[0]
    S = V7X_SUBLANES
    tT = min(512, T)
    assert T % tT == 0 and tT % S == 0 and nw - 1 <= S and cblk * nblk == C
    c0 = jnp.concatenate([jnp.zeros((B, S - (nw - 1), C), F32), conv0], axis=1)
    hg, hl = pl.pallas_call(
        functools.partial(_rglru_seq_kernel, tT=tT),
        grid=(B, nblk, T // tT),
        in_specs=[
            pl.BlockSpec((1, tT, cblk), lambda b, n, t: (b, t, n)),
            pl.BlockSpec((1, tT, cblk), lambda b, n, t: (b, t, n + nblk)),
            pl.BlockSpec((1, 1, cblk), lambda b, n, t: (b, 0, n)),
            pl.BlockSpec((1, S, cblk), lambda b, n, t: (b, 0, n)),
            pl.BlockSpec((nw, cblk), lambda b, n, t: (0, n)),
            pl.BlockSpec((1, cblk), lambda b, n, t: (0, n)),
            pl.BlockSpec((2, 1, cblk, cblk), lambda b, n, t: (0, n, 0, 0)),
            pl.BlockSpec((2, cblk), lambda b, n, t: (0, n)),
            pl.BlockSpec((1, cblk), lambda b, n, t: (0, n)),
        ],
        out_specs=[pl.BlockSpec((1, tT, cblk), lambda b, n, t: (b, t, n)),
                   pl.BlockSpec((1, 1, cblk), lambda b, n, t: (b, 0, n))],
        out_shape=[jax.ShapeDtypeStruct((B, T, C), BF16), jax.ShapeDtypeStruct((B, 1, C), F32)],
        scratch_shapes=[pltpu.VMEM((tT + S, cblk), F32), pltpu.VMEM((1, cblk), F32)],
        compiler_params=_params("parallel", "parallel", "arbitrary"),
        name="rglru_seq",
    )(xy, xy, h0.reshape(B, 1, C), c0, conv_w, conv_b.reshape(1, C), gate_w, gate_b, lam.reshape(1, C))
    return hg, hl.reshape(B, C)


def _rglru_step_kernel(x_ref, y_ref, h0_ref, c0_ref, cw_ref, cb_ref, gw_ref, gb_ref, lam_ref,
                       hg_ref, hl_ref):
    x = x_ref[...]
    cw = cw_ref[...]
    nw = cw.shape[0]
    conv = cb_ref[...] + cw[nw - 1:nw, :] * x
    for j in range(1, nw):
        conv = conv + cw[nw - 1 - j:nw - j, :] * c0_ref[nw - 1 - j]
    a, b = _rglru_coeffs(conv, gw_ref, gb_ref, lam_ref)
    h = a * h0_ref[...] + b
    hl_ref[...] = h
    hg_ref[...] = (h * _gelu_tanh(y_ref[...])).astype(hg_ref.dtype)


def _rglru_step(xy, h0, conv0, conv_w, conv_b, gate_w, gate_b, lam):
    B, C2 = xy.shape
    C = C2 // 2
    nblk, cblk = gate_w.shape[1], gate_w.shape[2]
    nw = conv_w.shape[0]
    c0 = jnp.swapaxes(conv0, 0, 1)
    return pl.pallas_call(
        _rglru_step_kernel,
        grid=(nblk,),
        in_specs=[
            pl.BlockSpec((B, cblk), lambda n: (0, n)),
            pl.BlockSpec((B, cblk), lambda n: (0, n + nblk)),
            pl.BlockSpec((B, cblk), lambda n: (0, n)),
            pl.BlockSpec((nw - 1, B, cblk), lambda n: (0, 0, n)),
            pl.BlockSpec((nw, cblk), lambda n: (0, n)),
            pl.BlockSpec((1, cblk), lambda n: (0, n)),
            pl.BlockSpec((2, 1, cblk, cblk), lambda n: (0, n, 0, 0)),
            pl.BlockSpec((2, cblk), lambda n: (0, n)),
            pl.BlockSpec((1, cblk), lambda n: (0, n)),
        ],
        out_specs=[pl.BlockSpec((B, cblk), lambda n: (0, n)), pl.BlockSpec((B, cblk), lambda n: (0, n))],
        out_shape=[jax.ShapeDtypeStruct((B, C), BF16), jax.ShapeDtypeStruct((B, C), F32)],
        compiler_params=_params("parallel"),
        name="rglru_step",
    )(xy, xy, h0, c0, conv_w, conv_b.reshape(1, C), gate_w, gate_b, lam.reshape(1, C))


def _tri_dots(lp, carry_col_ref, carry_row_ref, cum_col_ref, cum_row_ref):
    P = lp.shape[0]
    hi = lax.Precision.HIGHEST
    r = lax.broadcasted_iota(jnp.int32, (P, P), 0)
    c = lax.broadcasted_iota(jnp.int32, (P, P), 1)
    ones = jnp.ones((P, P), F32)
    tril = (c <= r).astype(F32)
    cum_col_ref[0] = jnp.dot(tril, lp, precision=hi, preferred_element_type=F32) + carry_col_ref[...]
    carry_col_ref[...] += jnp.dot(ones, lp, precision=hi, preferred_element_type=F32)
    lpt = lp.T
    triu = (r <= c).astype(F32)
    cum_row_ref[0] = jnp.dot(lpt, triu, precision=hi, preferred_element_type=F32) + carry_row_ref[...]
    carry_row_ref[...] += jnp.dot(lpt, ones, precision=hi, preferred_element_type=F32)


def _gate_logits(f, n_heads):
    return jnp.where(lax.broadcasted_iota(jnp.int32, f.shape, f.ndim - 1) < n_heads, f, 0.0)


def _logf_cum_kernel(f_ref, bf_ref, lf_ref, cc_ref, cr_ref, car_c, car_r, *, n_heads):
    @pl.when(pl.program_id(1) == 0)
    def _():
        car_c[...] = jnp.zeros_like(car_c)
        car_r[...] = jnp.zeros_like(car_r)

    lp = _log_sigmoid(_gate_logits(f_ref[0], n_heads) + bf_ref[...])
    lf_ref[0] = lp
    _tri_dots(lp, car_c, car_r, cc_ref, cr_ref)


def _logf_cum(f, b_f_pad, n_heads):
    B, T, L = f.shape
    P = V7X_LANES
    assert T % P == 0 and L == P
    return pl.pallas_call(
        functools.partial(_logf_cum_kernel, n_heads=n_heads),
        grid=(B, T // P),
        in_specs=[pl.BlockSpec((1, P, L), lambda b, t: (b, t, 0)), pl.BlockSpec((1, L), lambda b, t: (0, 0))],
        out_specs=[pl.BlockSpec((1, P, L), lambda b, t: (b, t, 0)),
                   pl.BlockSpec((1, P, L), lambda b, t: (b, t, 0)),
                   pl.BlockSpec((1, L, P), lambda b, t: (b, 0, t))],
        out_shape=[jax.ShapeDtypeStruct((B, T, L), F32), jax.ShapeDtypeStruct((B, T, L), F32),
                   jax.ShapeDtypeStruct((B, L, T), F32)],
        scratch_shapes=[pltpu.VMEM((P, L), F32), pltpu.VMEM((L, P), F32)],
        compiler_params=_params("parallel", "arbitrary"),
        name="logf_cum",
    )(f, b_f_pad)


def _paged_cum_kernel(pt_ref, *refs, n_in):
    lf_refs = refs[:n_in]
    cc_ref, tot_ref, car = refs[n_in:]

    @pl.when(pl.program_id(1) == 0)
    def _():
        car[...] = jnp.zeros_like(car)

    P = lf_refs[0].shape[1]
    hi = lax.Precision.HIGHEST
    tril = (lax.broadcasted_iota(jnp.int32, (P, P), 1) <= lax.broadcasted_iota(jnp.int32, (P, P), 0)).astype(F32)
    ones = jnp.ones((P, P), F32)
    for g, lf_ref in enumerate(lf_refs):
        lp = lf_ref[0]
        cc_ref[0, g * P:(g + 1) * P, :] = jnp.dot(tril, lp, precision=hi, preferred_element_type=F32) + car[...]
        car[...] += jnp.dot(ones, lp, precision=hi, preferred_element_type=F32)
    tot_ref[0] = car[0:V7X_SUBLANES, :]


def _paged_cum(cache_logf, page_table):
    n_pool, P, H = cache_logf.shape
    DB, n_pages = page_table.shape
    G = math.gcd(n_pages, 8)
    S = V7X_SUBLANES
    specs = [pl.BlockSpec((1, P, H), lambda b, s, pt, g=g: (pt[b, s * G + g], 0, 0)) for g in range(G)]
    cum, tot = pl.pallas_call(
        functools.partial(_paged_cum_kernel, n_in=G),
        grid_spec=pltpu.PrefetchScalarGridSpec(
            num_scalar_prefetch=1,
            grid=(DB, n_pages // G),
            in_specs=specs,
            out_specs=[pl.BlockSpec((1, G * P, H), lambda b, s, pt: (b, s, 0)),
                       pl.BlockSpec((1, S, H), lambda b, s, pt: (b, 0, 0))],
            scratch_shapes=[pltpu.VMEM((P, H), F32)],
        ),
        out_shape=[jax.ShapeDtypeStruct((DB, n_pages * P, H), F32), jax.ShapeDtypeStruct((DB, S, H), F32)],
        compiler_params=_params("parallel", "arbitrary"),
        name="paged_cum",
    )(page_table, *([cache_logf] * G))
    return cum, tot[:, 0, :]


FOX_HEADS_PER_STEP = 4


def _fox_prompt_kernel(q_ref, k_ref, vt_ref, cq_ref, ck_ref, o_ref, *, tq, hd, hb, nq, scale):
    qi = pl.program_id(2)
    nt = (((1,), (1,)), ((), ()))
    below_or_on = (lax.broadcasted_iota(jnp.int32, (tq, 1), 0)
                   <= lax.broadcasted_iota(jnp.int32, (1, tq), 1))

    def q_block(c):
        lo = c * tq
        heads = [slice(j * hd, (j + 1) * hd) for j in range(hb)]
        scale2 = scale * LOG2E
        scores = []
        for j, cols in enumerate(heads):
            qj = q_ref[0, :, cols]
            cqj = cq_ref[0, 0, j:j + 1, :] * LOG2E
            sd = lax.dot_general(k_ref[0, lo:lo + tq, cols], qj, nt, preferred_element_type=F32) * scale2
            sd = sd + (cqj - ck_ref[0, 0, lo:lo + tq, j:j + 1] * LOG2E)
            sd = jnp.where(below_or_on, sd, -jnp.inf)
            m = jnp.max(sd, axis=0, keepdims=True)
            sp = None
            if c:
                sp = lax.dot_general(k_ref[0, 0:lo, cols], qj, nt, preferred_element_type=F32) * scale2
                sp = sp + (cqj - ck_ref[0, 0, 0:lo, j:j + 1] * LOG2E)
                m = jnp.maximum(m, jnp.max(sp, axis=0, keepdims=True))
            scores.append((sd, sp, m))
        probs = []
        for sd, sp, m in scores:
            pd = jnp.exp2(sd - m)
            l = jnp.sum(pd, axis=0, keepdims=True)
            pp = None
            if c:
                pp = jnp.exp2(sp - m)
                l = l + jnp.sum(pp, axis=0, keepdims=True)
                pp = pp.astype(BF16)
            probs.append((pd.astype(BF16), pp, l))
        for cols, (pd, pp, l) in zip(heads, probs):
            acc = jnp.dot(vt_ref[cols, lo:lo + tq], pd, preferred_element_type=F32)
            if c:
                acc = acc + jnp.dot(vt_ref[cols, 0:lo], pp, preferred_element_type=F32)
            o_ref[0, :, cols] = (acc / l).T.astype(o_ref.dtype)

    for c in range(nq):
        pl.when(qi == c)(functools.partial(q_block, c))


def _fox_prompt(q, k, vt, cum_col, cum_row, n_heads):
    B, T, D = q.shape
    hd = D // n_heads
    hb = min(FOX_HEADS_PER_STEP, n_heads)
    hg = n_heads // hb
    tq = min(256, T)
    assert n_heads % hb == 0 and T % tq == 0
    cq = cum_row[:, :n_heads, :].reshape(B, hg, hb, T)
    ck = jnp.transpose(cum_col[:, :, :n_heads].reshape(B, T, hg, hb), (0, 2, 1, 3))
    return pl.pallas_call(
        functools.partial(_fox_prompt_kernel, tq=tq, hd=hd, hb=hb, nq=T // tq, scale=hd ** -0.5),
        grid=(B, hg, T // tq),
        in_specs=[pl.BlockSpec((1, tq, hb * hd), lambda b, g, i: (b, i, g)),
                  pl.BlockSpec((1, T, hb * hd), lambda b, g, i: (b, 0, g)),
                  pl.BlockSpec((hb * hd, T), lambda b, g, i: (g, b)),
                  pl.BlockSpec((1, 1, hb, tq), lambda b, g, i: (b, g, 0, i)),
                  pl.BlockSpec((1, 1, T, hb), lambda b, g, i: (b, g, 0, 0))],
        out_specs=pl.BlockSpec((1, tq, hb * hd), lambda b, g, i: (b, i, g)),
        out_shape=jax.ShapeDtypeStruct((B, T, D), BF16),
        compiler_params=_params("parallel", "parallel", "arbitrary"),
        name="fox_prompt",
    )(q, k, vt, cq, ck)


DECODE_PAGES_PER_STEP = 4


def _fox_decode_kernel(pt_ref, q_ref, *refs, n_pg, scale):
    k_refs, v_refs = refs[:n_pg], refs[n_pg:2 * n_pg]
    ckf_ref, cqf_ref, cqb_ref, kn_ref, vn_ref, o_ref, m_ref, l_ref, acc_ref = refs[2 * n_pg:]
    p_idx = pl.program_id(1)
    H = q_ref.shape[1]

    @pl.when(p_idx == 0)
    def _():
        m_ref[...] = jnp.full_like(m_ref, -jnp.inf)
        l_ref[...] = jnp.zeros_like(l_ref)
        acc_ref[...] = jnp.zeros_like(acc_ref)

    qb = q_ref[0].astype(BF16)
    rows = k_refs[0].shape[1]
    own = (lax.broadcasted_iota(jnp.int32, (H, rows), 1) % H) == lax.broadcasted_iota(jnp.int32, (H, rows), 0)
    m, l, acc = m_ref[...], l_ref[...], acc_ref[...]
    for g in range(n_pg):
        kb = k_refs[g][0].astype(BF16)
        s = lax.dot_general(qb, kb, (((1,), (1,)), ((), ())), preferred_element_type=F32) * scale
        s = s + (cqf_ref[0] - ckf_ref[0, 0, g:g + 1, :])
        s = jnp.where(own, s, -jnp.inf)
        m_new = jnp.maximum(m, jnp.max(s, axis=1, keepdims=True))
        alpha = jnp.exp(m - m_new)
        p = jnp.exp(s - m_new[:, 0:1])
        l = alpha * l + jnp.sum(p, axis=1, keepdims=True)
        acc = alpha * acc + jnp.dot(p.astype(BF16), v_refs[g][0].astype(BF16), preferred_element_type=F32)
        m = m_new
    m_ref[...] = m
    l_ref[...] = l
    acc_ref[...] = acc

    @pl.when(p_idx == pl.num_programs(1) - 1)
    def _():
        qn = q_ref[0].astype(BF16).astype(F32)
        kn = kn_ref[0].astype(BF16).astype(F32)
        vn = vn_ref[0].astype(BF16).astype(F32)
        cq = cqb_ref[0]
        s_n = jnp.sum(qn * kn, axis=1, keepdims=True) * scale + (cq - cq)
        m_p = m_ref[...]
        m_f = jnp.maximum(m_p, s_n)
        al = jnp.exp(m_p - m_f)
        p_n = jnp.exp(s_n - m_f)
        l_f = al * l_ref[...] + p_n
        o = al * acc_ref[...] + p_n.astype(BF16).astype(F32) * vn
        o_ref[0] = (o / l_f).astype(o_ref.dtype)


def _fox_decode(q, k_new, v_new, cache_k, cache_v, page_table, cum_past, cum_new):
    DB, H, hd = q.shape
    n_pool, P = cache_k.shape[0], cache_k.shape[1]
    n_pages = page_table.shape[1]
    R = P * H
    assert hd == V7X_LANES
    k2 = cache_k.reshape(n_pool, R, hd)
    v2 = cache_v.reshape(n_pool, R, hd)
    G = math.gcd(n_pages, DECODE_PAGES_PER_STEP)
    ckf = cum_past.reshape(DB, n_pages // G, G, R)
    cqf = jnp.tile(cum_new, (1, P)).reshape(DB, 1, R)
    cqb = jnp.broadcast_to(cum_new[:, :, None], (DB, H, hd))
    per_seq = lambda shape: pl.BlockSpec((1,) + shape, lambda b, s, pt: (b, 0, 0))
    page = [pl.BlockSpec((1, R, hd), lambda b, s, pt, g=g: (pt[b, s * G + g], 0, 0)) for g in range(G)]
    return pl.pallas_call(
        functools.partial(_fox_decode_kernel, n_pg=G, scale=hd ** -0.5),
        grid_spec=pltpu.PrefetchScalarGridSpec(
            num_scalar_prefetch=1,
            grid=(DB, n_pages // G),
            in_specs=[per_seq((H, hd))] + page + page
                     + [pl.BlockSpec((1, 1, G, R), lambda b, s, pt: (b, s, 0, 0)),
                        per_seq((1, R)), per_seq((H, hd)), per_seq((H, hd)), per_seq((H, hd))],
            out_specs=per_seq((H, hd)),
            scratch_shapes=[pltpu.VMEM((H, hd), F32), pltpu.VMEM((H, hd), F32), pltpu.VMEM((H, hd), F32)],
        ),
        out_shape=jax.ShapeDtypeStruct((DB, H, hd), BF16),
        compiler_params=_params("parallel", "arbitrary"),
        name="fox_decode",
    )(page_table, q, *([k2] * G), *([v2] * G), ckf, cqf, cqb, k_new, v_new)


def _forward(x_prompt, x_sample, h0_s, conv0_s, cache_k, cache_v, cache_logf, page_table, p):
    B, T, D = x_prompt.shape
    DB, DT, _ = x_sample.shape
    assert DT == 1, "the sample group carries one new token per sequence"
    M = B * T
    depth = p["w_up"].shape[0]
    n_a = p["w_in"].shape[0]
    H = p["b_f"].shape[0]
    hd = p["w_q"].shape[2] // H
    d_attn = H * hd
    nw = p["conv_w"].shape[1]
    assert T >= nw - 1

    xp = x_prompt.reshape(M, D)
    xs = x_sample.reshape(DB, D)
    xnp = _norm_cast(xp, p["norm_mix_pre"][0])
    xns = _norm_cast(xs, p["norm_mix_pre"][0])
    hs_p, convs_p, hs_s, convs_s = [], [], [], []
    kv = None
    for l in range(depth):
        if l < n_a:
            xy_p, xy_s = _mm_fullk(xnp, p["w_in"], l, a2=xns, epilogue="f32")
            C = xy_p.shape[1] // 2
            args = (p["conv_w"][l], p["conv_b"][l], p["gate_w"][l], p["gate_b"][l], p["rg_lambda"][l])
            xy_p3 = xy_p.reshape(B, T, 2 * C)
            hg_p, h_p = _rglru_seq(xy_p3, jnp.zeros((B, C), F32), jnp.zeros((B, nw - 1, C), F32), *args)
            hg_s, h_s = _rglru_step(xy_s, h0_s[l], conv0_s[l], *args)
            hs_p.append(h_p)
            hs_s.append(h_s)
            convs_p.append(xy_p3[:, T - (nw - 1):, :C])
            convs_s.append(jnp.concatenate([conv0_s[l], xy_s[:, None, :C]], axis=1)[:, 1:])
            m_p, m_s = _mm_fullk(hg_p.reshape(M, C), p["w_out_a"], l, a2=hg_s, epilogue="f32")
        else:
            j = l - n_a
            q_p, q_s = _mm_fullk(xnp, p["w_q"], j, a2=xns, epilogue="bf16", epilogue2="f32")
            o_p = _fox_prompt(q_p.reshape(B, T, d_attn), kv["kb"].reshape(B, T, d_attn), kv["vt"],
                              kv["cum_col"], kv["cum_row"], H)
            o_s = _fox_decode(q_s.reshape(DB, H, hd), kv["k_s"].reshape(DB, H, hd), kv["v_s"].reshape(DB, H, hd),
                              cache_k, cache_v, page_table, kv["cum_past"], kv["cum_new"])
            m_p, m_s = _mm_fullk(o_p.reshape(M, d_attn), p["w_o"], j, a2=o_s.reshape(DB, d_attn), epilogue="f32")
        xp, (xn2p,) = _resid_norm(xp, m_p, p["norm_mix_post"][l], [p["norm_mlp_pre"][l]])
        xs, (xn2s,) = _resid_norm(xs, m_s, p["norm_mix_post"][l], [p["norm_mlp_pre"][l]])
        hmid_p, hmid_s = _mm_fullk(xn2p, p["w_up"], l, a2=xn2s, epilogue="relu2_bf16")
        f_p, f_s = _mm_ktiled(hmid_p, p["w_down"], l, a2=hmid_s)
        if l + 1 == n_a:
            gains = [p["norm_kv"], p["norm_mix_pre"][l + 1]]
            xp, (xkv_p, xnp) = _resid_norm(xp, f_p, p["norm_mlp_post"][l], gains)
            xs, (xkv_s, xns) = _resid_norm(xs, f_s, p["norm_mlp_post"][l], gains)
            kv = _shared_kv(xkv_p, xkv_s, B, T, H, hd, cache_logf, page_table, p)
        elif l + 1 < depth:
            xp, (xnp,) = _resid_norm(xp, f_p, p["norm_mlp_post"][l], [p["norm_mix_pre"][l + 1]])
            xs, (xns,) = _resid_norm(xs, f_s, p["norm_mlp_post"][l], [p["norm_mix_pre"][l + 1]])
        else:
            xp, _ = _resid_norm(xp, f_p, p["norm_mlp_post"][l], [])
            xs, _ = _resid_norm(xs, f_s, p["norm_mlp_post"][l], [])
    return (xp.reshape(B, T, D), xs.reshape(DB, DT, D),
            jnp.stack(hs_p), jnp.stack(convs_p),
            kv["k_p"].reshape(B, T, H, hd), kv["v_p"].reshape(B, T, H, hd), kv["logf_p"],
            jnp.stack(hs_s), jnp.stack(convs_s),
            kv["k_s"].reshape(DB, DT, H, hd), kv["v_s"].reshape(DB, DT, H, hd), kv["logf_s"])


def _shared_kv(xkv_p, xkv_s, B, T, H, hd, cache_logf, page_table, p):
    d_attn = H * hd
    DB = xkv_s.shape[0]
    L = V7X_LANES
    w_kvf = p["w_kvf"]
    tn = min(512, d_attn)
    (k_p, kb), k_s = _mm_fullk(xkv_p, w_kvf, a2=xkv_s, n_cols=d_attn, col_block0=0,
                               epilogue="f32_bf16", epilogue2="f32")
    (v_p, vt), v_s = _mm_fullk(xkv_p, w_kvf, a2=xkv_s, n_cols=d_attn, col_block0=d_attn // tn,
                               epilogue="f32_bf16t", epilogue2="f32")
    assert (2 * d_attn) % L == 0 and H <= L
    b_f = jnp.pad(p["b_f"], (0, L - H)).reshape(1, L)
    f_p, f_s = _mm_fullk(xkv_p, w_kvf, a2=xkv_s, n_cols=L, col_block0=2 * d_attn // L, epilogue="f32")
    logf_p, cum_col, cum_row = _logf_cum(f_p.reshape(B, T, L), b_f, H)
    cum_past, tot = _paged_cum(cache_logf, page_table)
    logf_s, cum_new = _logf_new(f_s, b_f, tot)
    return dict(k_p=k_p, v_p=v_p, kb=kb, vt=vt, logf_p=logf_p[:, :, :H], cum_col=cum_col, cum_row=cum_row,
                k_s=k_s, v_s=v_s, logf_s=logf_s[:, :H].reshape(DB, 1, H), cum_past=cum_past, cum_new=cum_new)


def _logf_new_kernel(f_ref, bf_ref, tot_ref, lf_ref, cn_ref):
    H = tot_ref.shape[1]
    lp = _log_sigmoid(_gate_logits(f_ref[...], H) + bf_ref[...])
    lf_ref[...] = lp
    cn_ref[...] = tot_ref[...] + lp[:, :H]


def _logf_new(f, b_f_pad, tot):
    DB, L = f.shape
    H = tot.shape[1]
    return pl.pallas_call(
        _logf_new_kernel,
        out_shape=[jax.ShapeDtypeStruct((DB, L), F32), jax.ShapeDtypeStruct((DB, H), F32)],
        compiler_params=pltpu.CompilerParams(vmem_limit_bytes=VMEM_LIMIT),
        name="logf_new",
    )(f, b_f_pad, tot)


def kernel(x_prompt, x_sample, state_rglru_h, state_conv, cache_k, cache_v, cache_logf, page_table,
           w_in, conv_w, conv_b, gate_w, gate_b, rg_lambda, w_out_a, norm_kv, w_kvf, b_f, w_q, w_o,
           norm_mix_pre, norm_mix_post, norm_mlp_pre, norm_mlp_post, w_up, w_down):
    p = dict(w_in=w_in, conv_w=conv_w, conv_b=conv_b, gate_w=gate_w, gate_b=gate_b, rg_lambda=rg_lambda,
             w_out_a=w_out_a, norm_kv=norm_kv, w_kvf=w_kvf, b_f=b_f, w_q=w_q, w_o=w_o,
             norm_mix_pre=norm_mix_pre, norm_mix_post=norm_mix_post, norm_mlp_pre=norm_mlp_pre,
             norm_mlp_post=norm_mlp_post, w_up=w_up, w_down=w_down)
    return _forward(x_prompt, x_sample, state_rglru_h, state_conv, cache_k, cache_v, cache_logf, page_table, p)
```

```python
import functools
import math

import jax
import jax.numpy as jnp
from jax import lax
from jax.experimental import pallas as pl
from jax.experimental.pallas import tpu as pltpu

F32 = jnp.float32
BF16 = jnp.bfloat16

RMS_EPS = 1e-6
LRU_C = 8.0
LOG2E = math.log2(math.e)

V7X_VMEM_BYTES = 64 * 1024 * 1024
V7X_LANES = 128
V7X_SUBLANES = 8
VMEM_LIMIT = V7X_VMEM_BYTES - 2 * 1024 * 1024
MM_SPILL_ALLOWANCE = 5 * 1024 * 1024
NORM_ROWS = 256


def _params(*sem):
    return pltpu.CompilerParams(dimension_semantics=sem, vmem_limit_bytes=VMEM_LIMIT)


def _rms_scale(x):
    return x * lax.rsqrt(jnp.mean(x * x, axis=-1, keepdims=True) + RMS_EPS)


def _log_sigmoid(x):
    return jnp.minimum(x, 0.0) - jnp.log1p(jnp.exp(-jnp.abs(x)))


def _gelu_tanh(x):
    c = math.sqrt(2.0 / math.pi)
    return 0.5 * x * (1.0 + jnp.tanh(c * (x + 0.044715 * (x * x * x))))


def _norm_cast_kernel(x_ref, g_ref, o_ref):
    o_ref[...] = (_rms_scale(x_ref[...]) * g_ref[...]).astype(o_ref.dtype)


def _norm_cast(x, g):
    M, D = x.shape
    tr = min(NORM_ROWS, M)
    return pl.pallas_call(
        _norm_cast_kernel,
        grid=(M // tr,),
        in_specs=[pl.BlockSpec((tr, D), lambda i: (i, 0)), pl.BlockSpec((1, D), lambda i: (0, 0))],
        out_specs=pl.BlockSpec((tr, D), lambda i: (i, 0)),
        out_shape=jax.ShapeDtypeStruct((M, D), BF16),
        compiler_params=_params("parallel"),
        name="norm_cast",
    )(x, g.reshape(1, D))


def _resid_norm_kernel(x_ref, m_ref, gp_ref, gn_ref, xo_ref, *n_refs):
    xn = x_ref[...] + _rms_scale(m_ref[...]) * gp_ref[...]
    xo_ref[...] = xn
    if n_refs:
        y = _rms_scale(xn)
        for j, r in enumerate(n_refs):
            r[...] = (y * gn_ref[j:j + 1, :]).astype(r.dtype)


def _resid_norm(x, m, g_post, g_next):
    M, D = x.shape
    n = len(g_next)
    tr = min(NORM_ROWS, M)
    gn = jnp.stack(g_next) if n else jnp.zeros((1, D), F32)
    row = pl.BlockSpec((tr, D), lambda i: (i, 0))
    outs = pl.pallas_call(
        _resid_norm_kernel,
        grid=(M // tr,),
        in_specs=[row, row, pl.BlockSpec((1, D), lambda i: (0, 0)),
                  pl.BlockSpec((gn.shape[0], D), lambda i: (0, 0))],
        out_specs=[row] * (1 + n),
        out_shape=[jax.ShapeDtypeStruct((M, D), F32)] + [jax.ShapeDtypeStruct((M, D), BF16)] * n,
        compiler_params=_params("parallel"),
        name="resid_norm",
    )(x, m, g_post.reshape(1, D), gn)
    return outs[0], list(outs[1:])


def _mm_store(acc, o_refs, epilogue):
    if epilogue == "f32":
        o_refs[0][...] = acc
    elif epilogue == "bf16":
        o_refs[0][...] = acc.astype(BF16)
    elif epilogue == "f32_bf16":
        o_refs[0][...] = acc
        o_refs[1][...] = acc.astype(BF16)
    elif epilogue == "f32_bf16t":
        o_refs[0][...] = acc
        o_refs[1][...] = acc.T.astype(BF16)
    elif epilogue == "relu2_bf16":
        h = jnp.maximum(acc, 0.0)
        o_refs[0][...] = (h * h).astype(BF16)
    else:
        raise ValueError(epilogue)


MM_FULLK_TILE = (2048, 512)
MM_KTILED_TILE = (2048, 1024, 2048)
MM_KTILED_OUT_BUFFERS = 1


def _fullk_rows(M, K, tn, out_dtypes):
    tm = min(MM_FULLK_TILE[0], M)
    while True:
        need = (2 * tm * K * 2 + 2 * K * tn * 4 + K * tn * 2
                + sum(2 * tm * tn * jnp.dtype(d).itemsize for d in out_dtypes) + MM_SPILL_ALLOWANCE)
        if need <= VMEM_LIMIT or tm % 2 or tm <= V7X_SUBLANES:
            return tm
        tm //= 2

_EPILOGUE_DTYPES = {"f32": (F32,), "bf16": (BF16,), "f32_bf16": (F32, BF16), "f32_bf16t": (F32, BF16),
                    "relu2_bf16": (BF16,)}


def _mm_fullk_kernel(*refs, epilogue, epilogue2):
    n1 = len(_EPILOGUE_DTYPES[epilogue])
    if epilogue2 is None:
        a_ref, w_ref, *o_refs, wb_ref = refs
    else:
        a_ref, a2_ref, w_ref, *o_refs, wb_ref = refs

    @pl.when(pl.program_id(1) == 0)
    def _():
        wb_ref[...] = w_ref[...].astype(BF16)
        if epilogue2 is not None:
            acc2 = jnp.dot(a2_ref[...], wb_ref[...], preferred_element_type=F32)
            _mm_store(acc2, o_refs[n1:], epilogue2)

    acc = jnp.dot(a_ref[...], wb_ref[...], preferred_element_type=F32)
    _mm_store(acc, o_refs[:n1], epilogue)


def _mm_fullk(a, w, layer=0, *, a2=None, n_cols=None, col_block0=0, epilogue="f32", epilogue2=None):
    if w.ndim == 2:
        w = w[None]
    M, K = a.shape
    N = w.shape[2] if n_cols is None else n_cols
    dts = _EPILOGUE_DTYPES[epilogue]
    tn = min(MM_FULLK_TILE[1], N)
    tm = _fullk_rows(M, K, tn, dts)
    assert M % tm == 0 and N % tn == 0
    in_specs = [pl.BlockSpec((tm, K), lambda n, m: (m, 0))]
    out_specs = [pl.BlockSpec((tm, tn), lambda n, m: (m, n))] * len(dts)
    out_shape = [jax.ShapeDtypeStruct((M, N), d) for d in dts]
    if epilogue == "f32_bf16t":
        out_specs[1] = pl.BlockSpec((tn, tm), lambda n, m: (n, m))
        out_shape[1] = jax.ShapeDtypeStruct((N, M), dts[1])
    args = [a]
    if a2 is not None:
        epilogue2 = epilogue2 or epilogue
        M2 = a2.shape[0]
        dts2 = _EPILOGUE_DTYPES[epilogue2]
        in_specs.append(pl.BlockSpec((M2, K), lambda n, m: (0, 0)))
        out_specs += [pl.BlockSpec((M2, tn), lambda n, m: (0, n))] * len(dts2)
        out_shape += [jax.ShapeDtypeStruct((M2, N), d) for d in dts2]
        args.append(a2)
    in_specs.append(pl.BlockSpec((None, K, tn), lambda n, m: (layer, 0, n + col_block0)))
    outs = pl.pallas_call(
        functools.partial(_mm_fullk_kernel, epilogue=epilogue, epilogue2=epilogue2 if a2 is not None else None),
        grid=(N // tn, M // tm),
        in_specs=in_specs,
        out_specs=out_specs,
        out_shape=out_shape,
        scratch_shapes=[pltpu.VMEM((K, tn), BF16)],
        compiler_params=_params("arbitrary", "arbitrary"),
        name="mm_fullk_" + epilogue,
    )(*args, w)
    unpack = lambda o: o if len(o) > 1 else o[0]
    if a2 is None:
        return unpack(outs)
    return unpack(outs[:len(dts)]), unpack(outs[len(dts):])


def _mm_ktiled_kernel(*refs, with2):
    if with2:
        a_ref, a2_ref, w_ref, o_ref, o2_ref = refs
    else:
        a_ref, w_ref, o_ref = refs
    first_k = pl.program_id(2) == 0
    wb = w_ref[...].astype(BF16)
    part = jnp.dot(a_ref[...], wb, preferred_element_type=F32)

    @pl.when(first_k)
    def _():
        o_ref[...] = part

    @pl.when(jnp.logical_not(first_k))
    def _():
        o_ref[...] += part

    if with2:
        first_m = pl.program_id(0) == 0

        @pl.when(first_m & first_k)
        def _():
            o2_ref[...] = jnp.dot(a2_ref[...], wb, preferred_element_type=F32)

        @pl.when(first_m & jnp.logical_not(first_k))
        def _():
            o2_ref[...] += jnp.dot(a2_ref[...], wb, preferred_element_type=F32)


def _mm_ktiled(a, w, layer, a2=None):
    M, K = a.shape
    N = w.shape[2]
    tm = min(MM_KTILED_TILE[0], M)
    tn = min(MM_KTILED_TILE[1], N)
    tk = min(MM_KTILED_TILE[2], K)
    assert M % tm == 0 and N % tn == 0 and K % tk == 0
    in_specs = [pl.BlockSpec((tm, tk), lambda m, n, k: (m, k))]
    out_specs = [pl.BlockSpec((tm, tn), lambda m, n, k: (m, n), pipeline_mode=pl.Buffered(MM_KTILED_OUT_BUFFERS))]
    out_shape = [jax.ShapeDtypeStruct((M, N), F32)]
    args = [a]
    if a2 is not None:
        M2 = a2.shape[0]
        last_n, last_k = N // tn - 1, K // tk - 1
        in_specs.append(pl.BlockSpec((M2, tk), lambda m, n, k: (0, jnp.where(m == 0, k, last_k))))
        out_specs.append(pl.BlockSpec((M2, tn), lambda m, n, k: (0, jnp.where(m == 0, n, last_n))))
        out_shape.append(jax.ShapeDtypeStruct((M2, N), F32))
        args.append(a2)
    in_specs.append(pl.BlockSpec((None, tk, tn), lambda m, n, k: (layer, k, n)))
    outs = pl.pallas_call(
        functools.partial(_mm_ktiled_kernel, with2=a2 is not None),
        grid=(M // tm, N // tn, K // tk),
        in_specs=in_specs,
        out_specs=out_specs,
        out_shape=out_shape,
        compiler_params=_params("arbitrary", "arbitrary", "arbitrary"),
        name="mm_ktiled",
    )(*args, w)
    return outs[0] if a2 is None else (outs[0], outs[1])


def _rglru_coeffs(conv, gw_ref, gb_ref, lam_ref):
    cb = conv.astype(BF16)
    gi = jnp.dot(cb, gw_ref[0, 0].astype(BF16), preferred_element_type=F32) + gb_ref[0:1, :]
    gr = jnp.dot(cb, gw_ref[1, 0].astype(BF16), preferred_element_type=F32) + gb_ref[1:2, :]
    i_gate = jax.nn.sigmoid(gi)
    r_gate = jax.nn.sigmoid(gr)
    log_a = LRU_C * r_gate * _log_sigmoid(lam_ref[...])
    a = jnp.exp(log_a)
    th = jnp.tanh(log_a)
    mult = jnp.sqrt(-2.0 * th / (1.0 - th))
    return a, mult * i_gate * conv


def _rglru_seq_kernel(x_ref, y_ref, h0_ref, c0_ref, cw_ref, cb_ref, gw_ref, gb_ref, lam_ref,
                      hg_ref, hl_ref, xs_ref, hc_ref, *, tT):
    t = pl.program_id(2)
    S = V7X_SUBLANES

    @pl.when(t == 0)
    def _():
        xs_ref[0:S, :] = c0_ref[0]
        hc_ref[...] = h0_ref[0]

    x = x_ref[0]
    xs_ref[S:S + tT, :] = x
    cw = cw_ref[...]
    nw = cw.shape[0]
    conv = cb_ref[...] + cw[nw - 1:nw, :] * x
    for j in range(1, nw):
        conv = conv + cw[nw - 1 - j:nw - j, :] * xs_ref[S - j:S - j + tT, :]
    xs_ref[0:S, :] = xs_ref[tT:tT + S, :]

    a, b = _rglru_coeffs(conv, gw_ref, gb_ref, lam_ref)

    row = lax.broadcasted_iota(jnp.int32, (S, a.shape[1]), 0)
    h = hc_ref[...]
    hs = []
    for r in range(0, tT, S):
        a8 = a[r:r + S]
        b8 = b[r:r + S]
        s = 1
        while s < S:
            keep = row >= s
            b8 = jnp.where(keep, a8 * pltpu.roll(b8, s, 0) + b8, b8)
            a8 = jnp.where(keep, a8 * pltpu.roll(a8, s, 0), a8)
            s *= 2
        h8 = a8 * h + b8
        hs.append(h8)
        h = h8[S - 1:S, :]
    hc_ref[...] = h
    hl_ref[0] = h
    hg_ref[0] = (jnp.concatenate(hs, axis=0) * _gelu_tanh(y_ref[0])).astype(hg_ref.dtype)


def _rglru_seq(xy, h0, conv0, conv_w, conv_b, gate_w, gate_b, lam):
    B, T, C2 = xy.shape
    C = C2 // 2
    nblk, cblk = gate_w.shape[1], gate_w.shape[2]
    nw = conv_w.shape[0]
    S = V7X_SUBLANES
    tT = min(512, T)
    assert T % tT == 0 and tT % S == 0 and nw - 1 <= S and cblk * nblk == C
    c0 = jnp.concatenate([jnp.zeros((B, S - (nw - 1), C), F32), conv0], axis=1)
    hg, hl = pl.pallas_call(
        functools.partial(_rglru_seq_kernel, tT=tT),
        grid=(B, nblk, T // tT),
        in_specs=[
            pl.BlockSpec((1, tT, cblk), lambda b, n, t: (b, t, n)),
            pl.BlockSpec((1, tT, cblk), lambda b, n, t: (b, t, n + nblk)),
            pl.BlockSpec((1, 1, cblk), lambda b, n, t: (b, 0, n)),
            pl.BlockSpec((1, S, cblk), lambda b, n, t: (b, 0, n)),
            pl.BlockSpec((nw, cblk), lambda b, n, t: (0, n)),
            pl.BlockSpec((1, cblk), lambda b, n, t: (0, n)),
            pl.BlockSpec((2, 1, cblk, cblk), lambda b, n, t: (0, n, 0, 0)),
            pl.BlockSpec((2, cblk), lambda b, n, t: (0, n)),
            pl.BlockSpec((1, cblk), lambda b, n, t: (0, n)),
        ],
        out_specs=[pl.BlockSpec((1, tT, cblk), lambda b, n, t: (b, t, n)),
                   pl.BlockSpec((1, 1, cblk), lambda b, n, t: (b, 0, n))],
        out_shape=[jax.ShapeDtypeStruct((B, T, C), BF16), jax.ShapeDtypeStruct((B, 1, C), F32)],
        scratch_shapes=[pltpu.VMEM((tT + S, cblk), F32), pltpu.VMEM((1, cblk), F32)],
        compiler_params=_params("parallel", "parallel", "arbitrary"),
        name="rglru_seq",
    )(xy, xy, h0.reshape(B, 1, C), c0, conv_w, conv_b.reshape(1, C), gate_w, gate_b, lam.reshape(1, C))
    return hg, hl.reshape(B, C)


def _rglru_step_kernel(x_ref, y_ref, h0_ref, c0_ref, cw_ref, cb_ref, gw_ref, gb_ref, lam_ref,
                       hg_ref, hl_ref):
    x = x_ref[...]
    cw = cw_ref[...]
    nw = cw.shape[0]
    conv = cb_ref[...] + cw[nw - 1:nw, :] * x
    for j in range(1, nw):
        conv = conv + cw[nw - 1 - j:nw - j, :] * c0_ref[nw - 1 - j]
    a, b = _rglru_coeffs(conv, gw_ref, gb_ref, lam_ref)
    h = a * h0_ref[...] + b
    hl_ref[...] = h
    hg_ref[...] = (h * _gelu_tanh(y_ref[...])).astype(hg_ref.dtype)


def _rglru_step(xy, h0, conv0, conv_w, conv_b, gate_w, gate_b, lam):
    B, C2 = xy.shape
    C = C2 // 2
    nblk, cblk = gate_w.shape[1], gate_w.shape[2]
    nw = conv_w.shape[0]
    c0 = jnp.swapaxes(conv0, 0, 1)
    return pl.pallas_call(
        _rglru_step_kernel,
        grid=(nblk,),
        in_specs=[
            pl.BlockSpec((B, cblk), lambda n: (0, n)),
            pl.BlockSpec((B, cblk), lambda n: (0, n + nblk)),
            pl.BlockSpec((B, cblk), lambda n: (0, n)),
            pl.BlockSpec((nw - 1, B, cblk), lambda n: (0, 0, n)),
            pl.BlockSpec((nw, cblk), lambda n: (0, n)),
            pl.BlockSpec((1, cblk), lambda n: (0, n)),
            pl.BlockSpec((2, 1, cblk, cblk), lambda n: (0, n, 0, 0)),
            pl.BlockSpec((2, cblk), lambda n: (0, n)),
            pl.BlockSpec((1, cblk), lambda n: (0, n)),
        ],
        out_specs=[pl.BlockSpec((B, cblk), lambda n: (0, n)), pl.BlockSpec((B, cblk), lambda n: (0, n))],
        out_shape=[jax.ShapeDtypeStruct((B, C), BF16), jax.ShapeDtypeStruct((B, C), F32)],
        compiler_params=_params("parallel"),
        name="rglru_step",
    )(xy, xy, h0, c0, conv_w, conv_b.reshape(1, C), gate_w, gate_b, lam.reshape(1, C))


def _tri_dots(lp, carry_col_ref, carry_row_ref, cum_col_ref, cum_row_ref):
    P = lp.shape[0]
    hi = lax.Precision.HIGHEST
    r = lax.broadcasted_iota(jnp.int32, (P, P), 0)
    c = lax.broadcasted_iota(jnp.int32, (P, P), 1)
    ones = jnp.ones((P, P), F32)
    tril = (c <= r).astype(F32)
    cum_col_ref[0] = jnp.dot(tril, lp, precision=hi, preferred_element_type=F32) + carry_col_ref[...]
    carry_col_ref[...] += jnp.dot(ones, lp, precision=hi, preferred_element_type=F32)
    lpt = lp.T
    triu = (r <= c).astype(F32)
    cum_row_ref[0] = jnp.dot(lpt, triu, precision=hi, preferred_element_type=F32) + carry_row_ref[...]
    carry_row_ref[...] += jnp.dot(lpt, ones, precision=hi, preferred_element_type=F32)


def _gate_logits(f, n_heads):
    return jnp.where(lax.broadcasted_iota(jnp.int32, f.shape, f.ndim - 1) < n_heads, f, 0.0)


def _logf_cum_kernel(f_ref, bf_ref, lf_ref, cc_ref, cr_ref, car_c, car_r, *, n_heads):
    @pl.when(pl.program_id(1) == 0)
    def _():
        car_c[...] = jnp.zeros_like(car_c)
        car_r[...] = jnp.zeros_like(car_r)

    lp = _log_sigmoid(_gate_logits(f_ref[0], n_heads) + bf_ref[...])
    lf_ref[0] = lp
    _tri_dots(lp, car_c, car_r, cc_ref, cr_ref)


def _logf_cum(f, b_f_pad, n_heads):
    B, T, L = f.shape
    P = V7X_LANES
    assert T % P == 0 and L == P
    return pl.pallas_call(
        functools.partial(_logf_cum_kernel, n_heads=n_heads),
        grid=(B, T // P),
        in_specs=[pl.BlockSpec((1, P, L), lambda b, t: (b, t, 0)), pl.BlockSpec((1, L), lambda b, t: (0, 0))],
        out_specs=[pl.BlockSpec((1, P, L), lambda b, t: (b, t, 0)),
                   pl.BlockSpec((1, P, L), lambda b, t: (b, t, 0)),
                   pl.BlockSpec((1, L, P), lambda b, t: (b, 0, t))],
        out_shape=[jax.ShapeDtypeStruct((B, T, L), F32), jax.ShapeDtypeStruct((B, T, L), F32),
                   jax.ShapeDtypeStruct((B, L, T), F32)],
        scratch_shapes=[pltpu.VMEM((P, L), F32), pltpu.VMEM((L, P), F32)],
        compiler_params=_params("parallel", "arbitrary"),
        name="logf_cum",
    )(f, b_f_pad)


def _paged_cum_kernel(pt_ref, *refs, n_in):
    lf_refs = refs[:n_in]
    cc_ref, tot_ref, car = refs[n_in:]

    @pl.when(pl.program_id(1) == 0)
    def _():
        car[...] = jnp.zeros_like(car)

    P = lf_refs[0].shape[1]
    hi = lax.Precision.HIGHEST
    tril = (lax.broadcasted_iota(jnp.int32, (P, P), 1) <= lax.broadcasted_iota(jnp.int32, (P, P), 0)).astype(F32)
    for g, lf_ref in enumerate(lf_refs):
        lp = lf_ref[0]
        cc_ref[0, g * P:(g + 1) * P, :] = jnp.dot(tril, lp, precision=hi, preferred_element_type=F32) + car[...]
        car[...] += jnp.sum(lp, axis=0, keepdims=True)
    tot_ref[0] = car[0:V7X_SUBLANES, :]


def _paged_cum(cache_logf, page_table):
    n_pool, P, H = cache_logf.shape
    DB, n_pages = page_table.shape
    G = math.gcd(n_pages, 8)
    S = V7X_SUBLANES
    specs = [pl.BlockSpec((1, P, H), lambda b, s, pt, g=g: (pt[b, s * G + g], 0, 0)) for g in range(G)]
    cum, tot = pl.pallas_call(
        functools.partial(_paged_cum_kernel, n_in=G),
        grid_spec=pltpu.PrefetchScalarGridSpec(
            num_scalar_prefetch=1,
            grid=(DB, n_pages // G),
            in_specs=specs,
            out_specs=[pl.BlockSpec((1, G * P, H), lambda b, s, pt: (b, s, 0)),
                       pl.BlockSpec((1, S, H), lambda b, s, pt: (b, 0, 0))],
            scratch_shapes=[pltpu.VMEM((P, H), F32)],
        ),
        out_shape=[jax.ShapeDtypeStruct((DB, n_pages * P, H), F32), jax.ShapeDtypeStruct((DB, S, H), F32)],
        compiler_params=_params("parallel", "arbitrary"),
        name="paged_cum",
    )(page_table, *([cache_logf] * G))
    return cum, tot[:, 0, :]


FOX_HEADS_PER_STEP = 8


def _fox_prompt_kernel(q_ref, k_ref, vt_ref, cq_ref, ck_ref, o_ref, *, tq, hd, hb, nq, scale):
    qi = pl.program_id(2)
    nt = (((1,), (1,)), ((), ()))
    below_or_on = (lax.broadcasted_iota(jnp.int32, (tq, 1), 0)
                   <= lax.broadcasted_iota(jnp.int32, (1, tq), 1))

    def q_block(c):
        lo = c * tq
        heads = [slice(j * hd, (j + 1) * hd) for j in range(hb)]
        scale2 = scale * LOG2E
        scores = []
        for j, cols in enumerate(heads):
            qj = q_ref[0, :, cols]
            cqj = cq_ref[0, 0, j:j + 1, :] * LOG2E
            sd = lax.dot_general(k_ref[0, lo:lo + tq, cols], qj, nt, preferred_element_type=F32) * scale2
            sd = sd + (cqj - ck_ref[0, 0, lo:lo + tq, j:j + 1] * LOG2E)
            sd = jnp.where(below_or_on, sd, -jnp.inf)
            m = jnp.max(sd, axis=0, keepdims=True)
            sp = None
            if c:
                sp = lax.dot_general(k_ref[0, 0:lo, cols], qj, nt, preferred_element_type=F32) * scale2
                sp = sp + (cqj - ck_ref[0, 0, 0:lo, j:j + 1] * LOG2E)
                m = jnp.maximum(m, jnp.max(sp, axis=0, keepdims=True))
            scores.append((sd, sp, m))
        probs = []
        for sd, sp, m in scores:
            pd = jnp.exp2(sd - m)
            l = jnp.sum(pd, axis=0, keepdims=True)
            pp = None
            if c:
                pp = jnp.exp2(sp - m)
                l = l + jnp.sum(pp, axis=0, keepdims=True)
                pp = pp.astype(BF16)
            probs.append((pd.astype(BF16), pp, l))
        for cols, (pd, pp, l) in zip(heads, probs):
            acc = jnp.dot(vt_ref[cols, lo:lo + tq], pd, preferred_element_type=F32)
            if c:
                acc = acc + jnp.dot(vt_ref[cols, 0:lo], pp, preferred_element_type=F32)
            o_ref[0, :, cols] = (acc / l).T.astype(o_ref.dtype)

    for c in range(nq):
        pl.when(qi == c)(functools.partial(q_block, c))


def _fox_prompt(q, k, vt, cum_col, cum_row, n_heads):
    B, T, D = q.shape
    hd = D // n_heads
    hb = min(FOX_HEADS_PER_STEP, n_heads)
    hg = n_heads // hb
    tq = min(256, T)
    assert n_heads % hb == 0 and T % tq == 0
    cq = cum_row[:, :n_heads, :].reshape(B, hg, hb, T)
    ck = jnp.transpose(cum_col[:, :, :n_heads].reshape(B, T, hg, hb), (0, 2, 1, 3))
    return pl.pallas_call(
        functools.partial(_fox_prompt_kernel, tq=tq, hd=hd, hb=hb, nq=T // tq, scale=hd ** -0.5),
        grid=(B, hg, T // tq),
        in_specs=[pl.BlockSpec((1, tq, hb * hd), lambda b, g, i: (b, i, g)),
                  pl.BlockSpec((1, T, hb * hd), lambda b, g, i: (b, 0, g)),
                  pl.BlockSpec((hb * hd, T), lambda b, g, i: (g, b)),
                  pl.BlockSpec((1, 1, hb, tq), lambda b, g, i: (b, g, 0, i)),
                  pl.BlockSpec((1, 1, T, hb), lambda b, g, i: (b, g, 0, 0))],
        out_specs=pl.BlockSpec((1, tq, hb * hd), lambda b, g, i: (b, i, g)),
        out_shape=jax.ShapeDtypeStruct((B, T, D), BF16),
        compiler_params=_params("parallel", "parallel", "arbitrary"),
        name="fox_prompt",
    )(q, k, vt, cq, ck)


DECODE_PAGES_PER_STEP = 4


def _fox_decode_kernel(pt_ref, q_ref, *refs, n_pg, scale):
    k_refs, v_refs = refs[:n_pg], refs[n_pg:2 * n_pg]
    ckf_ref, cqf_ref, cqb_ref, kn_ref, vn_ref, o_ref, m_ref, l_ref, acc_ref = refs[2 * n_pg:]
    p_idx = pl.program_id(1)
    H = q_ref.shape[1]

    @pl.when(p_idx == 0)
    def _():
        m_ref[...] = jnp.full_like(m_ref, -jnp.inf)
        l_ref[...] = jnp.zeros_like(l_ref)
        acc_ref[...] = jnp.zeros_like(acc_ref)

    qb = q_ref[0].astype(BF16)
    rows = k_refs[0].shape[1]
    own = (lax.broadcasted_iota(jnp.int32, (H, rows), 1) % H) == lax.broadcasted_iota(jnp.int32, (H, rows), 0)
    m, l, acc = m_ref[...], l_ref[...], acc_ref[...]
    for g in range(n_pg):
        kb = k_refs[g][0].astype(BF16)
        s = lax.dot_general(qb, kb, (((1,), (1,)), ((), ())), preferred_element_type=F32) * scale
        s = s + (cqf_ref[0] - ckf_ref[0, 0, g:g + 1, :])
        s = jnp.where(own, s, -jnp.inf)
        m_new = jnp.maximum(m, jnp.max(s, axis=1, keepdims=True))
        alpha = jnp.exp(m - m_new)
        p = jnp.exp(s - m_new[:, 0:1])
        l = alpha * l + jnp.sum(p, axis=1, keepdims=True)
        acc = alpha * acc + jnp.dot(p.astype(BF16), v_refs[g][0].astype(BF16), preferred_element_type=F32)
        m = m_new
    m_ref[...] = m
    l_ref[...] = l
    acc_ref[...] = acc

    @pl.when(p_idx == pl.num_programs(1) - 1)
    def _():
        qn = q_ref[0].astype(BF16).astype(F32)
        kn = kn_ref[0].astype(BF16).astype(F32)
        vn = vn_ref[0].astype(BF16).astype(F32)
        cq = cqb_ref[0]
        s_n = jnp.sum(qn * kn, axis=1, keepdims=True) * scale + (cq - cq)
        m_p = m_ref[...]
        m_f = jnp.maximum(m_p, s_n)
        al = jnp.exp(m_p - m_f)
        p_n = jnp.exp(s_n - m_f)
        l_f = al * l_ref[...] + p_n
        o = al * acc_ref[...] + p_n.astype(BF16).astype(F32) * vn
        o_ref[0] = (o / l_f).astype(o_ref.dtype)


def _fox_decode(q, k_new, v_new, cache_k, cache_v, page_table, cum_past, cum_new):
    DB, H, hd = q.shape
    n_pool, P = cache_k.shape[0], cache_k.shape[1]
    n_pages = page_table.shape[1]
    R = P * H
    assert hd == V7X_LANES
    k2 = cache_k.reshape(n_pool, R, hd)
    v2 = cache_v.reshape(n_pool, R, hd)
    G = math.gcd(n_pages, DECODE_PAGES_PER_STEP)
    ckf = cum_past.reshape(DB, n_pages // G, G, R)
    cqf = jnp.tile(cum_new, (1, P)).reshape(DB, 1, R)
    cqb = jnp.broadcast_to(cum_new[:, :, None], (DB, H, hd))
    per_seq = lambda shape: pl.BlockSpec((1,) + shape, lambda b, s, pt: (b, 0, 0))
    page = [pl.BlockSpec((1, R, hd), lambda b, s, pt, g=g: (pt[b, s * G + g], 0, 0)) for g in range(G)]
    return pl.pallas_call(
        functools.partial(_fox_decode_kernel, n_pg=G, scale=hd ** -0.5),
        grid_spec=pltpu.PrefetchScalarGridSpec(
            num_scalar_prefetch=1,
            grid=(DB, n_pages // G),
            in_specs=[per_seq((H, hd))] + page + page
                     + [pl.BlockSpec((1, 1, G, R), lambda b, s, pt: (b, s, 0, 0)),
                        per_seq((1, R)), per_seq((H, hd)), per_seq((H, hd)), per_seq((H, hd))],
            out_specs=per_seq((H, hd)),
            scratch_shapes=[pltpu.VMEM((H, hd), F32), pltpu.VMEM((H, hd), F32), pltpu.VMEM((H, hd), F32)],
        ),
        out_shape=jax.ShapeDtypeStruct((DB, H, hd), BF16),
        compiler_params=_params("parallel", "arbitrary"),
        name="fox_decode",
    )(page_table, q, *([k2] * G), *([v2] * G), ckf, cqf, cqb, k_new, v_new)


def _forward(x_prompt, x_sample, h0_s, conv0_s, cache_k, cache_v, cache_logf, page_table, p):
    B, T, D = x_prompt.shape
    DB, DT, _ = x_sample.shape
    assert DT == 1, "the sample group carries one new token per sequence"
    M = B * T
    depth = p["w_up"].shape[0]
    n_a = p["w_in"].shape[0]
    H = p["b_f"].shape[0]
    hd = p["w_q"].shape[2] // H
    d_attn = H * hd
    nw = p["conv_w"].shape[1]
    assert T >= nw - 1

    xp = x_prompt.reshape(M, D)
    xs = x_sample.reshape(DB, D)
    xnp = _norm_cast(xp, p["norm_mix_pre"][0])
    xns = _norm_cast(xs, p["norm_mix_pre"][0])
    hs_p, convs_p, hs_s, convs_s = [], [], [], []
    kv = None
    for l in range(depth):
        if l < n_a:
            xy_p, xy_s = _mm_fullk(xnp, p["w_in"], l, a2=xns, epilogue="f32")
            C = xy_p.shape[1] // 2
            args = (p["conv_w"][l], p["conv_b"][l], p["gate_w"][l], p["gate_b"][l], p["rg_lambda"][l])
            xy_p3 = xy_p.reshape(B, T, 2 * C)
            hg_p, h_p = _rglru_seq(xy_p3, jnp.zeros((B, C), F32), jnp.zeros((B, nw - 1, C), F32), *args)
            hg_s, h_s = _rglru_step(xy_s, h0_s[l], conv0_s[l], *args)
            hs_p.append(h_p)
            hs_s.append(h_s)
            convs_p.append(xy_p3[:, T - (nw - 1):, :C])
            convs_s.append(jnp.concatenate([conv0_s[l], xy_s[:, None, :C]], axis=1)[:, 1:])
            m_p, m_s = _mm_fullk(hg_p.reshape(M, C), p["w_out_a"], l, a2=hg_s, epilogue="f32")
        else:
            j = l - n_a
            q_p, q_s = _mm_fullk(xnp, p["w_q"], j, a2=xns, epilogue="bf16", epilogue2="f32")
            o_p = _fox_prompt(q_p.reshape(B, T, d_attn), kv["kb"].reshape(B, T, d_attn), kv["vt"],
                              kv["cum_col"], kv["cum_row"], H)
            o_s = _fox_decode(q_s.reshape(DB, H, hd), kv["k_s"].reshape(DB, H, hd), kv["v_s"].reshape(DB, H, hd),
                              cache_k, cache_v, page_table, kv["cum_past"], kv["cum_new"])
            m_p, m_s = _mm_fullk(o_p.reshape(M, d_attn), p["w_o"], j, a2=o_s.reshape(DB, d_attn), epilogue="f32")
        xp, (xn2p,) = _resid_norm(xp, m_p, p["norm_mix_post"][l], [p["norm_mlp_pre"][l]])
        xs, (xn2s,) = _resid_norm(xs, m_s, p["norm_mix_post"][l], [p["norm_mlp_pre"][l]])
        hmid_p, hmid_s = _mm_fullk(xn2p, p["w_up"], l, a2=xn2s, epilogue="relu2_bf16")
        f_p, f_s = _mm_ktiled(hmid_p, p["w_down"], l, a2=hmid_s)
        if l + 1 == n_a:
            gains = [p["norm_kv"], p["norm_mix_pre"][l + 1]]
            xp, (xkv_p, xnp) = _resid_norm(xp, f_p, p["norm_mlp_post"][l], gains)
            xs, (xkv_s, xns) = _resid_norm(xs, f_s, p["norm_mlp_post"][l], gains)
            kv = _shared_kv(xkv_p, xkv_s, B, T, H, hd, cache_logf, page_table, p)
        elif l + 1 < depth:
            xp, (xnp,) = _resid_norm(xp, f_p, p["norm_mlp_post"][l], [p["norm_mix_pre"][l + 1]])
            xs, (xns,) = _resid_norm(xs, f_s, p["norm_mlp_post"][l], [p["norm_mix_pre"][l + 1]])
        else:
            xp, _ = _resid_norm(xp, f_p, p["norm_mlp_post"][l], [])
            xs, _ = _resid_norm(xs, f_s, p["norm_mlp_post"][l], [])
    return (xp.reshape(B, T, D), xs.reshape(DB, DT, D),
            jnp.stack(hs_p), jnp.stack(convs_p),
            kv["k_p"].reshape(B, T, H, hd), kv["v_p"].reshape(B, T, H, hd), kv["logf_p"],
            jnp.stack(hs_s), jnp.stack(convs_s),
            kv["k_s"].reshape(DB, DT, H, hd), kv["v_s"].reshape(DB, DT, H, hd), kv["logf_s"])


def _shared_kv(xkv_p, xkv_s, B, T, H, hd, cache_logf, page_table, p):
    d_attn = H * hd
    DB = xkv_s.shape[0]
    L = V7X_LANES
    w_kvf = p["w_kvf"]
    tn = min(512, d_attn)
    (k_p, kb), k_s = _mm_fullk(xkv_p, w_kvf, a2=xkv_s, n_cols=d_attn, col_block0=0,
                               epilogue="f32_bf16", epilogue2="f32")
    (v_p, vt), v_s = _mm_fullk(xkv_p, w_kvf, a2=xkv_s, n_cols=d_attn, col_block0=d_attn // tn,
                               epilogue="f32_bf16t", epilogue2="f32")
    assert (2 * d_attn) % L == 0 and H <= L
    b_f = jnp.pad(p["b_f"], (0, L - H)).reshape(1, L)
    f_p, f_s = _mm_fullk(xkv_p, w_kvf, a2=xkv_s, n_cols=L, col_block0=2 * d_attn // L, epilogue="f32")
    logf_p, cum_col, cum_row = _logf_cum(f_p.reshape(B, T, L), b_f, H)
    cum_past, tot = _paged_cum(cache_logf, page_table)
    logf_s, cum_new = _logf_new(f_s, b_f, tot)
    return dict(k_p=k_p, v_p=v_p, kb=kb, vt=vt, logf_p=logf_p[:, :, :H], cum_col=cum_col, cum_row=cum_row,
                k_s=k_s, v_s=v_s, logf_s=logf_s[:, :H].reshape(DB, 1, H), cum_past=cum_past, cum_new=cum_new)


def _logf_new_kernel(f_ref, bf_ref, tot_ref, lf_ref, cn_ref):
    H = tot_ref.shape[1]
    lp = _log_sigmoid(_gate_logits(f_ref[...], H) + bf_ref[...])
    lf_ref[...] = lp
    cn_ref[...] = tot_ref[...] + lp[:, :H]


def _logf_new(f, b_f_pad, tot):
    DB, L = f.shape
    H = tot.shape[1]
    return pl.pallas_call(
        _logf_new_kernel,
        out_shape=[jax.ShapeDtypeStruct((DB, L), F32), jax.ShapeDtypeStruct((DB, H), F32)],
        compiler_params=pltpu.CompilerParams(vmem_limit_bytes=VMEM_LIMIT),
        name="logf_new",
    )(f, b_f_pad, tot)


def kernel(x_prompt, x_sample, state_rglru_h, state_conv, cache_k, cache_v, cache_logf, page_table,
           w_in, conv_w, conv_b, gate_w, gate_b, rg_lambda, w_out_a, norm_kv, w_kvf, b_f, w_q, w_o,
           norm_mix_pre, norm_mix_post, norm_mlp_pre, norm_mlp_post, w_up, w_down):
    p = dict(w_in=w_in, conv_w=conv_w, conv_b=conv_b, gate_w=gate_w, gate_b=gate_b, rg_lambda=rg_lambda,
             w_out_a=w_out_a, norm_kv=norm_kv, w_kvf=w_kvf, b_f=b_f, w_q=w_q, w_o=w_o,
             norm_mix_pre=norm_mix_pre, norm_mix_post=norm_mix_post, norm_mlp_pre=norm_mlp_pre,
             norm_mlp_post=norm_mlp_post, w_up=w_up, w_down=w_down)
    return _forward(x_prompt, x_sample, state_rglru_h, state_conv, cache_k, cache_v, cache_logf, page_table, p)
```

```python
import functools
import math

import jax
import jax.numpy as jnp
from jax import lax
from jax.experimental import pallas as pl
from jax.experimental.pallas import tpu as pltpu

F32 = jnp.float32
BF16 = jnp.bfloat16

RMS_EPS = 1e-6
LRU_C = 8.0
LOG2E = math.log2(math.e)

V7X_VMEM_BYTES = 64 * 1024 * 1024
V7X_LANES = 128
V7X_SUBLANES = 8
VMEM_LIMIT = V7X_VMEM_BYTES - 2 * 1024 * 1024
MM_SPILL_ALLOWANCE = 5 * 1024 * 1024
NORM_ROWS = 256


def _params(*sem):
    return pltpu.CompilerParams(dimension_semantics=sem, vmem_limit_bytes=VMEM_LIMIT)


def _rms_scale(x):
    return x * lax.rsqrt(jnp.mean(x * x, axis=-1, keepdims=True) + RMS_EPS)


def _log_sigmoid(x):
    return jnp.minimum(x, 0.0) - jnp.log1p(jnp.exp(-jnp.abs(x)))


def _gelu_tanh(x):
    c = math.sqrt(2.0 / math.pi)
    return 0.5 * x * (1.0 + jnp.tanh(c * (x + 0.044715 * (x * x * x))))


def _norm_cast_kernel(x_ref, g_ref, o_ref):
    o_ref[...] = (_rms_scale(x_ref[...]) * g_ref[...]).astype(o_ref.dtype)


def _norm_cast(x, g):
    M, D = x.shape
    tr = min(NORM_ROWS, M)
    return pl.pallas_call(
        _norm_cast_kernel,
        grid=(M // tr,),
        in_specs=[pl.BlockSpec((tr, D), lambda i: (i, 0)), pl.BlockSpec((1, D), lambda i: (0, 0))],
        out_specs=pl.BlockSpec((tr, D), lambda i: (i, 0)),
        out_shape=jax.ShapeDtypeStruct((M, D), BF16),
        compiler_params=_params("parallel"),
        name="norm_cast",
    )(x, g.reshape(1, D))


def _resid_norm_kernel(x_ref, m_ref, gp_ref, gn_ref, xo_ref, *n_refs):
    xn = x_ref[...] + _rms_scale(m_ref[...]) * gp_ref[...]
    xo_ref[...] = xn
    if n_refs:
        y = _rms_scale(xn)
        for j, r in enumerate(n_refs):
            r[...] = (y * gn_ref[j:j + 1, :]).astype(r.dtype)


def _resid_norm(x, m, g_post, g_next):
    M, D = x.shape
    n = len(g_next)
    tr = min(NORM_ROWS, M)
    gn = jnp.stack(g_next) if n else jnp.zeros((1, D), F32)
    row = pl.BlockSpec((tr, D), lambda i: (i, 0))
    outs = pl.pallas_call(
        _resid_norm_kernel,
        grid=(M // tr,),
        in_specs=[row, row, pl.BlockSpec((1, D), lambda i: (0, 0)),
                  pl.BlockSpec((gn.shape[0], D), lambda i: (0, 0))],
        out_specs=[row] * (1 + n),
        out_shape=[jax.ShapeDtypeStruct((M, D), F32)] + [jax.ShapeDtypeStruct((M, D), BF16)] * n,
        compiler_params=_params("parallel"),
        name="resid_norm",
    )(x, m, g_post.reshape(1, D), gn)
    return outs[0], list(outs[1:])


def _mm_store(acc, o_refs, epilogue):
    if epilogue == "f32":
        o_refs[0][...] = acc
    elif epilogue == "bf16":
        o_refs[0][...] = acc.astype(BF16)
    elif epilogue == "f32_bf16":
        o_refs[0][...] = acc
        o_refs[1][...] = acc.astype(BF16)
    elif epilogue == "f32_bf16t":
        o_refs[0][...] = acc
        o_refs[1][...] = acc.T.astype(BF16)
    elif epilogue == "relu2_bf16":
        h = jnp.maximum(acc, 0.0)
        o_refs[0][...] = (h * h).astype(BF16)
    else:
        raise ValueError(epilogue)


MM_FULLK_TILE = (2048, 512)
MM_KTILED_TILE = (2048, 1024, 2048)
MM_KTILED_OUT_BUFFERS = 1


def _fullk_rows(M, K, tn, out_dtypes):
    tm = min(MM_FULLK_TILE[0], M)
    while True:
        need = (2 * tm * K * 2 + 2 * K * tn * 4 + K * tn * 2
                + sum(2 * tm * tn * jnp.dtype(d).itemsize for d in out_dtypes) + MM_SPILL_ALLOWANCE)
        if need <= VMEM_LIMIT or tm % 2 or tm <= V7X_SUBLANES:
            return tm
        tm //= 2

_EPILOGUE_DTYPES = {"f32": (F32,), "bf16": (BF16,), "f32_bf16": (F32, BF16), "f32_bf16t": (F32, BF16),
                    "relu2_bf16": (BF16,)}


def _mm_fullk_kernel(*refs, epilogue, epilogue2, w_t):
    n1 = len(_EPILOGUE_DTYPES[epilogue])
    if epilogue2 is None:
        a_ref, w_ref, *o_refs, wb_ref = refs
    else:
        a_ref, a2_ref, w_ref, *o_refs, wb_ref = refs
    dims = (((1,), (1 if w_t else 0,)), ((), ()))

    @pl.when(pl.program_id(1) == 0)
    def _():
        wb_ref[...] = w_ref[...].astype(BF16)
        if epilogue2 is not None:
            acc2 = lax.dot_general(a2_ref[...], wb_ref[...], dims, preferred_element_type=F32)
            _mm_store(acc2, o_refs[n1:], epilogue2)

    acc = lax.dot_general(a_ref[...], wb_ref[...], dims, preferred_element_type=F32)
    _mm_store(acc, o_refs[:n1], epilogue)


def _mm_fullk(a, w, layer=0, *, a2=None, n_cols=None, col_block0=0, epilogue="f32", epilogue2=None,
              w_t=False):
    if w.ndim == 2:
        w = w[None]
    M, K = a.shape
    N = w.shape[1 if w_t else 2] if n_cols is None else n_cols
    dts = _EPILOGUE_DTYPES[epilogue]
    tn = min(MM_FULLK_TILE[1], N)
    tm = _fullk_rows(M, K, tn, dts)
    assert M % tm == 0 and N % tn == 0
    in_specs = [pl.BlockSpec((tm, K), lambda n, m: (m, 0))]
    out_specs = [pl.BlockSpec((tm, tn), lambda n, m: (m, n))] * len(dts)
    out_shape = [jax.ShapeDtypeStruct((M, N), d) for d in dts]
    if epilogue == "f32_bf16t":
        out_specs[1] = pl.BlockSpec((tn, tm), lambda n, m: (n, m))
        out_shape[1] = jax.ShapeDtypeStruct((N, M), dts[1])
    args = [a]
    if a2 is not None:
        epilogue2 = epilogue2 or epilogue
        M2 = a2.shape[0]
        dts2 = _EPILOGUE_DTYPES[epilogue2]
        in_specs.append(pl.BlockSpec((M2, K), lambda n, m: (0, 0)))
        out_specs += [pl.BlockSpec((M2, tn), lambda n, m: (0, n))] * len(dts2)
        out_shape += [jax.ShapeDtypeStruct((M2, N), d) for d in dts2]
        args.append(a2)
    if w_t:
        in_specs.append(pl.BlockSpec((None, tn, K), lambda n, m: (layer, n + col_block0, 0)))
    else:
        in_specs.append(pl.BlockSpec((None, K, tn), lambda n, m: (layer, 0, n + col_block0)))
    outs = pl.pallas_call(
        functools.partial(_mm_fullk_kernel, epilogue=epilogue, epilogue2=epilogue2 if a2 is not None else None,
                          w_t=w_t),
        grid=(N // tn, M // tm),
        in_specs=in_specs,
        out_specs=out_specs,
        out_shape=out_shape,
        scratch_shapes=[pltpu.VMEM((tn, K) if w_t else (K, tn), BF16)],
        compiler_params=_params("arbitrary", "arbitrary"),
        name="mm_fullk_" + epilogue,
    )(*args, w)
    unpack = lambda o: o if len(o) > 1 else o[0]
    if a2 is None:
        return unpack(outs)
    return unpack(outs[:len(dts)]), unpack(outs[len(dts):])


def _mm_ktiled_kernel(*refs, with2):
    if with2:
        a_ref, a2_ref, w_ref, o_ref, o2_ref = refs
    else:
        a_ref, w_ref, o_ref = refs
    first_k = pl.program_id(2) == 0
    wb = w_ref[...].astype(BF16)
    part = jnp.dot(a_ref[...], wb, preferred_element_type=F32)

    @pl.when(first_k)
    def _():
        o_ref[...] = part

    @pl.when(jnp.logical_not(first_k))
    def _():
        o_ref[...] += part

    if with2:
        first_m = pl.program_id(0) == 0

        @pl.when(first_m & first_k)
        def _():
            o2_ref[...] = jnp.dot(a2_ref[...], wb, preferred_element_type=F32)

        @pl.when(first_m & jnp.logical_not(first_k))
        def _():
            o2_ref[...] += jnp.dot(a2_ref[...], wb, preferred_element_type=F32)


def _mm_ktiled(a, w, layer, a2=None):
    M, K = a.shape
    N = w.shape[2]
    tm = min(MM_KTILED_TILE[0], M)
    tn = min(MM_KTILED_TILE[1], N)
    tk = min(MM_KTILED_TILE[2], K)
    assert M % tm == 0 and N % tn == 0 and K % tk == 0
    in_specs = [pl.BlockSpec((tm, tk), lambda m, n, k: (m, k))]
    out_specs = [pl.BlockSpec((tm, tn), lambda m, n, k: (m, n), pipeline_mode=pl.Buffered(MM_KTILED_OUT_BUFFERS))]
    out_shape = [jax.ShapeDtypeStruct((M, N), F32)]
    args = [a]
    if a2 is not None:
        M2 = a2.shape[0]
        last_n, last_k = N // tn - 1, K // tk - 1
        in_specs.append(pl.BlockSpec((M2, tk), lambda m, n, k: (0, jnp.where(m == 0, k, last_k))))
        out_specs.append(pl.BlockSpec((M2, tn), lambda m, n, k: (0, jnp.where(m == 0, n, last_n))))
        out_shape.append(jax.ShapeDtypeStruct((M2, N), F32))
        args.append(a2)
    in_specs.append(pl.BlockSpec((None, tk, tn), lambda m, n, k: (layer, k, n)))
    outs = pl.pallas_call(
        functools.partial(_mm_ktiled_kernel, with2=a2 is not None),
        grid=(M // tm, N // tn, K // tk),
        in_specs=in_specs,
        out_specs=out_specs,
        out_shape=out_shape,
        compiler_params=_params("arbitrary", "arbitrary", "arbitrary"),
        name="mm_ktiled",
    )(*args, w)
    return outs[0] if a2 is None else (outs[0], outs[1])


def _rglru_coeffs(conv, gw_ref, gb_ref, lam_ref):
    cb = conv.astype(BF16)
    gi = jnp.dot(cb, gw_ref[0, 0].astype(BF16), preferred_element_type=F32) + gb_ref[0:1, :]
    gr = jnp.dot(cb, gw_ref[1, 0].astype(BF16), preferred_element_type=F32) + gb_ref[1:2, :]
    i_gate = jax.nn.sigmoid(gi)
    r_gate = jax.nn.sigmoid(gr)
    log_a = LRU_C * r_gate * _log_sigmoid(lam_ref[...])
    a = jnp.exp(log_a)
    th = jnp.tanh(log_a)
    mult = jnp.sqrt(-2.0 * th / (1.0 - th))
    return a, mult * i_gate * conv


RGLRU_ROWS = 1024


def _rglru_seq_kernel(x_ref, y_ref, h0_ref, c0_ref, cw_ref, cb_ref, gw_ref, gb_ref, lam_ref,
                      hg_ref, hl_ref, xs_ref, hc_ref, *, tT):
    t = pl.program_id(2)
    S = V7X_SUBLANES

    @pl.when(t == 0)
    def _():
        xs_ref[0:S, :] = c0_ref[0]
        hc_ref[...] = h0_ref[0]

    x = x_ref[0]
    xs_ref[S:S + tT, :] = x
    cw = cw_ref[...]
    nw = cw.shape[0]
    conv = cb_ref[...] + cw[nw - 1:nw, :] * x
    for j in range(1, nw):
        conv = conv + cw[nw - 1 - j:nw - j, :] * xs_ref[S - j:S - j + tT, :]
    xs_ref[0:S, :] = xs_ref[tT:tT + S, :]

    a, b = _rglru_coeffs(conv, gw_ref, gb_ref, lam_ref)

    row = lax.broadcasted_iota(jnp.int32, (S, a.shape[1]), 0)
    h = hc_ref[...]
    hs = []
    for r in range(0, tT, S):
        a8 = a[r:r + S]
        b8 = b[r:r + S]
        s = 1
        while s < S:
            keep = row >= s
            b8 = jnp.where(keep, a8 * pltpu.roll(b8, s, 0) + b8, b8)
            a8 = jnp.where(keep, a8 * pltpu.roll(a8, s, 0), a8)
            s *= 2
        h8 = a8 * h + b8
        hs.append(h8)
        h = h8[S - 1:S, :]
    hc_ref[...] = h
    hl_ref[0] = h
    hg_ref[0] = (jnp.concatenate(hs, axis=0) * _gelu_tanh(y_ref[0])).astype(hg_ref.dtype)


def _rglru_seq(xy, h0, conv0, conv_w, conv_b, gate_w, gate_b, lam):
    B, T, C2 = xy.shape
    C = C2 // 2
    nblk, cblk = gate_w.shape[1], gate_w.shape[2]
    nw = conv_w.shape[0]
    S = V7X_SUBLANES
    tT = min(RGLRU_ROWS, T)
    assert T % tT == 0 and tT % S == 0 and nw - 1 <= S and cblk * nblk == C
    c0 = jnp.concatenate([jnp.zeros((B, S - (nw - 1), C), F32), conv0], axis=1)
    hg, hl = pl.pallas_call(
        functools.partial(_rglru_seq_kernel, tT=tT),
        grid=(B, nblk, T // tT),
        in_specs=[
            pl.BlockSpec((1, tT, cblk), lambda b, n, t: (b, t, n)),
            pl.BlockSpec((1, tT, cblk), lambda b, n, t: (b, t, n + nblk)),
            pl.BlockSpec((1, 1, cblk), lambda b, n, t: (b, 0, n)),
            pl.BlockSpec((1, S, cblk), lambda b, n, t: (b, 0, n)),
            pl.BlockSpec((nw, cblk), lambda b, n, t: (0, n)),
            pl.BlockSpec((1, cblk), lambda b, n, t: (0, n)),
            pl.BlockSpec((2, 1, cblk, cblk), lambda b, n, t: (0, n, 0, 0)),
            pl.BlockSpec((2, cblk), lambda b, n, t: (0, n)),
            pl.BlockSpec((1, cblk), lambda b, n, t: (0, n)),
        ],
        out_specs=[pl.BlockSpec((1, tT, cblk), lambda b, n, t: (b, t, n)),
                   pl.BlockSpec((1, 1, cblk), lambda b, n, t: (b, 0, n))],
        out_shape=[jax.ShapeDtypeStruct((B, T, C), BF16), jax.ShapeDtypeStruct((B, 1, C), F32)],
        scratch_shapes=[pltpu.VMEM((tT + S, cblk), F32), pltpu.VMEM((1, cblk), F32)],
        compiler_params=_params("parallel", "parallel", "arbitrary"),
        name="rglru_seq",
    )(xy, xy, h0.reshape(B, 1, C), c0, conv_w, conv_b.reshape(1, C), gate_w, gate_b, lam.reshape(1, C))
    return hg, hl.reshape(B, C)


def _rglru_step_kernel(x_ref, y_ref, h0_ref, c0_ref, cw_ref, cb_ref, gw_ref, gb_ref, lam_ref,
                       hg_ref, hl_ref):
    x = x_ref[...]
    cw = cw_ref[...]
    nw = cw.shape[0]
    conv = cb_ref[...] + cw[nw - 1:nw, :] * x
    for j in range(1, nw):
        conv = conv + cw[nw - 1 - j:nw - j, :] * c0_ref[nw - 1 - j]
    a, b = _rglru_coeffs(conv, gw_ref, gb_ref, lam_ref)
    h = a * h0_ref[...] + b
    hl_ref[...] = h
    hg_ref[...] = (h * _gelu_tanh(y_ref[...])).astype(hg_ref.dtype)


def _rglru_step(xy, h0, conv0, conv_w, conv_b, gate_w, gate_b, lam):
    B, C2 = xy.shape
    C = C2 // 2
    nblk, cblk = gate_w.shape[1], gate_w.shape[2]
    nw = conv_w.shape[0]
    c0 = jnp.swapaxes(conv0, 0, 1)
    return pl.pallas_call(
        _rglru_step_kernel,
        grid=(nblk,),
        in_specs=[
            pl.BlockSpec((B, cblk), lambda n: (0, n)),
            pl.BlockSpec((B, cblk), lambda n: (0, n + nblk)),
            pl.BlockSpec((B, cblk), lambda n: (0, n)),
            pl.BlockSpec((nw - 1, B, cblk), lambda n: (0, 0, n)),
            pl.BlockSpec((nw, cblk), lambda n: (0, n)),
            pl.BlockSpec((1, cblk), lambda n: (0, n)),
            pl.BlockSpec((2, 1, cblk, cblk), lambda n: (0, n, 0, 0)),
            pl.BlockSpec((2, cblk), lambda n: (0, n)),
            pl.BlockSpec((1, cblk), lambda n: (0, n)),
        ],
        out_specs=[pl.BlockSpec((B, cblk), lambda n: (0, n)), pl.BlockSpec((B, cblk), lambda n: (0, n))],
        out_shape=[jax.ShapeDtypeStruct((B, C), BF16), jax.ShapeDtypeStruct((B, C), F32)],
        compiler_params=_params("parallel"),
        name="rglru_step",
    )(xy, xy, h0, c0, conv_w, conv_b.reshape(1, C), gate_w, gate_b, lam.reshape(1, C))


def _tri_dots(lp, carry_col_ref, carry_row_ref, cum_col_ref, cum_row_ref):
    P = lp.shape[0]
    hi = lax.Precision.HIGHEST
    r = lax.broadcasted_iota(jnp.int32, (P, P), 0)
    c = lax.broadcasted_iota(jnp.int32, (P, P), 1)
    ones = jnp.ones((P, P), F32)
    tril = (c <= r).astype(F32)
    cum_col_ref[0] = jnp.dot(tril, lp, precision=hi, preferred_element_type=F32) + carry_col_ref[...]
    carry_col_ref[...] += jnp.dot(ones, lp, precision=hi, preferred_element_type=F32)
    lpt = lp.T
    triu = (r <= c).astype(F32)
    cum_row_ref[0] = jnp.dot(lpt, triu, precision=hi, preferred_element_type=F32) + carry_row_ref[...]
    carry_row_ref[...] += jnp.dot(lpt, ones, precision=hi, preferred_element_type=F32)


def _gate_logits(f, n_heads):
    return jnp.where(lax.broadcasted_iota(jnp.int32, f.shape, f.ndim - 1) < n_heads, f, 0.0)


def _logf_cum_kernel(f_ref, bf_ref, lf_ref, cc_ref, cr_ref, car_c, car_r, *, n_heads):
    @pl.when(pl.program_id(1) == 0)
    def _():
        car_c[...] = jnp.zeros_like(car_c)
        car_r[...] = jnp.zeros_like(car_r)

    lp = _log_sigmoid(_gate_logits(f_ref[0], n_heads) + bf_ref[...])
    lf_ref[0] = lp
    _tri_dots(lp, car_c, car_r, cc_ref, cr_ref)


def _logf_cum(f, b_f_pad, n_heads):
    B, T, L = f.shape
    P = V7X_LANES
    assert T % P == 0 and L == P
    return pl.pallas_call(
        functools.partial(_logf_cum_kernel, n_heads=n_heads),
        grid=(B, T // P),
        in_specs=[pl.BlockSpec((1, P, L), lambda b, t: (b, t, 0)), pl.BlockSpec((1, L), lambda b, t: (0, 0))],
        out_specs=[pl.BlockSpec((1, P, L), lambda b, t: (b, t, 0)),
                   pl.BlockSpec((1, P, L), lambda b, t: (b, t, 0)),
                   pl.BlockSpec((1, L, P), lambda b, t: (b, 0, t))],
        out_shape=[jax.ShapeDtypeStruct((B, T, L), F32), jax.ShapeDtypeStruct((B, T, L), F32),
                   jax.ShapeDtypeStruct((B, L, T), F32)],
        scratch_shapes=[pltpu.VMEM((P, L), F32), pltpu.VMEM((L, P), F32)],
        compiler_params=_params("parallel", "arbitrary"),
        name="logf_cum",
    )(f, b_f_pad)


def _paged_cum_kernel(pt_ref, *refs, n_in):
    lf_refs = refs[:n_in]
    cc_ref, tot_ref, car = refs[n_in:]

    @pl.when(pl.program_id(1) == 0)
    def _():
        car[...] = jnp.zeros_like(car)

    P = lf_refs[0].shape[1]
    hi = lax.Precision.HIGHEST
    tril = (lax.broadcasted_iota(jnp.int32, (P, P), 1) <= lax.broadcasted_iota(jnp.int32, (P, P), 0)).astype(F32)
    for g, lf_ref in enumerate(lf_refs):
        lp = lf_ref[0]
        cc_ref[0, g * P:(g + 1) * P, :] = jnp.dot(tril, lp, precision=hi, preferred_element_type=F32) + car[...]
        car[...] += jnp.sum(lp, axis=0, keepdims=True)
    tot_ref[0] = car[0:V7X_SUBLANES, :]


def _paged_cum(cache_logf, page_table):
    n_pool, P, H = cache_logf.shape
    DB, n_pages = page_table.shape
    G = math.gcd(n_pages, 8)
    S = V7X_SUBLANES
    specs = [pl.BlockSpec((1, P, H), lambda b, s, pt, g=g: (pt[b, s * G + g], 0, 0)) for g in range(G)]
    cum, tot = pl.pallas_call(
        functools.partial(_paged_cum_kernel, n_in=G),
        grid_spec=pltpu.PrefetchScalarGridSpec(
            num_scalar_prefetch=1,
            grid=(DB, n_pages // G),
            in_specs=specs,
            out_specs=[pl.BlockSpec((1, G * P, H), lambda b, s, pt: (b, s, 0)),
                       pl.BlockSpec((1, S, H), lambda b, s, pt: (b, 0, 0))],
            scratch_shapes=[pltpu.VMEM((P, H), F32)],
        ),
        out_shape=[jax.ShapeDtypeStruct((DB, n_pages * P, H), F32), jax.ShapeDtypeStruct((DB, S, H), F32)],
        compiler_params=_params("parallel", "arbitrary"),
        name="paged_cum",
    )(page_table, *([cache_logf] * G))
    return cum, tot[:, 0, :]


FOX_HEADS_PER_STEP = 8


def _fox_prompt_kernel(q_ref, k_ref, vt_ref, cq_ref, ck_ref, o_ref, *, tq, hd, hb, nq, scale):
    qi = pl.program_id(2)
    nt = (((1,), (1,)), ((), ()))
    below_or_on = (lax.broadcasted_iota(jnp.int32, (tq, 1), 0)
                   <= lax.broadcasted_iota(jnp.int32, (1, tq), 1))

    def q_block(c):
        lo = c * tq
        heads = [slice(j * hd, (j + 1) * hd) for j in range(hb)]
        scale2 = scale * LOG2E
        scores = []
        for j, cols in enumerate(heads):
            qj = q_ref[0, :, cols]
            cqj = cq_ref[0, 0, j:j + 1, :] * LOG2E
            sd = lax.dot_general(k_ref[0, lo:lo + tq, cols], qj, nt, preferred_element_type=F32) * scale2
            sd = sd + (cqj - ck_ref[0, 0, lo:lo + tq, j:j + 1] * LOG2E)
            sd = jnp.where(below_or_on, sd, -jnp.inf)
            m = jnp.max(sd, axis=0, keepdims=True)
            sp = None
            if c:
                sp = lax.dot_general(k_ref[0, 0:lo, cols], qj, nt, preferred_element_type=F32) * scale2
                sp = sp + (cqj - ck_ref[0, 0, 0:lo, j:j + 1] * LOG2E)
                m = jnp.maximum(m, jnp.max(sp, axis=0, keepdims=True))
            scores.append((sd, sp, m))
        probs = []
        for sd, sp, m in scores:
            pd = jnp.exp2(sd - m)
            l = jnp.sum(pd, axis=0, keepdims=True)
            pp = None
            if c:
                pp = jnp.exp2(sp - m)
                l = l + jnp.sum(pp, axis=0, keepdims=True)
                pp = pp.astype(BF16)
            probs.append((pd.astype(BF16), pp, l))
        for cols, (pd, pp, l) in zip(heads, probs):
            acc = jnp.dot(vt_ref[cols, lo:lo + tq], pd, preferred_element_type=F32)
            if c:
                acc = acc + jnp.dot(vt_ref[cols, 0:lo], pp, preferred_element_type=F32)
            o_ref[0, :, cols] = (acc / l).T.astype(o_ref.dtype)

    for c in range(nq):
        pl.when(qi == c)(functools.partial(q_block, c))


def _fox_prompt(q, k, vt, cum_col, cum_row, n_heads):
    B, T, D = q.shape
    hd = D // n_heads
    hb = min(FOX_HEADS_PER_STEP, n_heads)
    hg = n_heads // hb
    tq = min(256, T)
    assert n_heads % hb == 0 and T % tq == 0
    cq = cum_row[:, :n_heads, :].reshape(B, hg, hb, T)
    ck = jnp.transpose(cum_col[:, :, :n_heads].reshape(B, T, hg, hb), (0, 2, 1, 3))
    return pl.pallas_call(
        functools.partial(_fox_prompt_kernel, tq=tq, hd=hd, hb=hb, nq=T // tq, scale=hd ** -0.5),
        grid=(B, hg, T // tq),
        in_specs=[pl.BlockSpec((1, tq, hb * hd), lambda b, g, i: (b, i, g)),
                  pl.BlockSpec((1, T, hb * hd), lambda b, g, i: (b, 0, g)),
                  pl.BlockSpec((hb * hd, T), lambda b, g, i: (g, b)),
                  pl.BlockSpec((1, 1, hb, tq), lambda b, g, i: (b, g, 0, i)),
                  pl.BlockSpec((1, 1, T, hb), lambda b, g, i: (b, g, 0, 0))],
        out_specs=pl.BlockSpec((1, tq, hb * hd), lambda b, g, i: (b, i, g)),
        out_shape=jax.ShapeDtypeStruct((B, T, D), BF16),
        compiler_params=_params("parallel", "parallel", "arbitrary"),
        name="fox_prompt",
    )(q, k, vt, cq, ck)


DECODE_PAGES_PER_STEP = 4


def _fox_decode_kernel(pt_ref, q_ref, *refs, n_pg, scale):
    k_refs, v_refs = refs[:n_pg], refs[n_pg:2 * n_pg]
    ckf_ref, cqf_ref, cqb_ref, kn_ref, vn_ref, o_ref, m_ref, l_ref, acc_ref = refs[2 * n_pg:]
    p_idx = pl.program_id(1)
    H = q_ref.shape[1]

    @pl.when(p_idx == 0)
    def _():
        m_ref[...] = jnp.full_like(m_ref, -jnp.inf)
        l_ref[...] = jnp.zeros_like(l_ref)
        acc_ref[...] = jnp.zeros_like(acc_ref)

    qb = q_ref[0].astype(BF16)
    rows = k_refs[0].shape[1]
    own = (lax.broadcasted_iota(jnp.int32, (H, rows), 1) % H) == lax.broadcasted_iota(jnp.int32, (H, rows), 0)
    m, l, acc = m_ref[...], l_ref[...], acc_ref[...]
    for g in range(n_pg):
        kb = k_refs[g][0].astype(BF16)
        s = lax.dot_general(qb, kb, (((1,), (1,)), ((), ())), preferred_element_type=F32) * scale
        s = s + (cqf_ref[0] - ckf_ref[0, 0, g:g + 1, :])
        s = jnp.where(own, s, -jnp.inf)
        m_new = jnp.maximum(m, jnp.max(s, axis=1, keepdims=True))
        alpha = jnp.exp(m - m_new)
        p = jnp.exp(s - m_new[:, 0:1])
        l = alpha * l + jnp.sum(p, axis=1, keepdims=True)
        acc = alpha * acc + jnp.dot(p.astype(BF16), v_refs[g][0].astype(BF16), preferred_element_type=F32)
        m = m_new
    m_ref[...] = m
    l_ref[...] = l
    acc_ref[...] = acc

    @pl.when(p_idx == pl.num_programs(1) - 1)
    def _():
        qn = q_ref[0].astype(BF16).astype(F32)
        kn = kn_ref[0].astype(BF16).astype(F32)
        vn = vn_ref[0].astype(BF16).astype(F32)
        cq = cqb_ref[0]
        s_n = jnp.sum(qn * kn, axis=1, keepdims=True) * scale + (cq - cq)
        m_p = m_ref[...]
        m_f = jnp.maximum(m_p, s_n)
        al = jnp.exp(m_p - m_f)
        p_n = jnp.exp(s_n - m_f)
        l_f = al * l_ref[...] + p_n
        o = al * acc_ref[...] + p_n.astype(BF16).astype(F32) * vn
        o_ref[0] = (o / l_f).astype(o_ref.dtype)


def _fox_decode(q, k_new, v_new, cache_k, cache_v, page_table, cum_past, cum_new):
    DB, H, hd = q.shape
    n_pool, P = cache_k.shape[0], cache_k.shape[1]
    n_pages = page_table.shape[1]
    R = P * H
    assert hd == V7X_LANES
    k2 = cache_k.reshape(n_pool, R, hd)
    v2 = cache_v.reshape(n_pool, R, hd)
    G = math.gcd(n_pages, DECODE_PAGES_PER_STEP)
    ckf = cum_past.reshape(DB, n_pages // G, G, R)
    cqf = jnp.tile(cum_new, (1, P)).reshape(DB, 1, R)
    cqb = jnp.broadcast_to(cum_new[:, :, None], (DB, H, hd))
    per_seq = lambda shape: pl.BlockSpec((1,) + shape, lambda b, s, pt: (b, 0, 0))
    page = [pl.BlockSpec((1, R, hd), lambda b, s, pt, g=g: (pt[b, s * G + g], 0, 0)) for g in range(G)]
    return pl.pallas_call(
        functools.partial(_fox_decode_kernel, n_pg=G, scale=hd ** -0.5),
        grid_spec=pltpu.PrefetchScalarGridSpec(
            num_scalar_prefetch=1,
            grid=(DB, n_pages // G),
            in_specs=[per_seq((H, hd))] + page + page
                     + [pl.BlockSpec((1, 1, G, R), lambda b, s, pt: (b, s, 0, 0)),
                        per_seq((1, R)), per_seq((H, hd)), per_seq((H, hd)), per_seq((H, hd))],
            out_specs=per_seq((H, hd)),
            scratch_shapes=[pltpu.VMEM((H, hd), F32), pltpu.VMEM((H, hd), F32), pltpu.VMEM((H, hd), F32)],
        ),
        out_shape=jax.ShapeDtypeStruct((DB, H, hd), BF16),
        compiler_params=_params("parallel", "arbitrary"),
        name="fox_decode",
    )(page_table, q, *([k2] * G), *([v2] * G), ckf, cqf, cqb, k_new, v_new)


def _forward(x_prompt, x_sample, h0_s, conv0_s, cache_k, cache_v, cache_logf, page_table, p):
    B, T, D = x_prompt.shape
    DB, DT, _ = x_sample.shape
    assert DT == 1, "the sample group carries one new token per sequence"
    M = B * T
    depth = p["w_up"].shape[0]
    n_a = p["w_in"].shape[0]
    H = p["b_f"].shape[0]
    hd = p["w_q"].shape[2] // H
    d_attn = H * hd
    nw = p["conv_w"].shape[1]
    assert T >= nw - 1

    xp = x_prompt.reshape(M, D)
    xs = x_sample.reshape(DB, D)
    xnp = _norm_cast(xp, p["norm_mix_pre"][0])
    xns = _norm_cast(xs, p["norm_mix_pre"][0])
    hs_p, convs_p, hs_s, convs_s = [], [], [], []
    kv = None
    for l in range(depth):
        if l < n_a:
            xy_p, xy_s = _mm_fullk(xnp, p["w_in"], l, a2=xns, epilogue="f32")
            C = xy_p.shape[1] // 2
            args = (p["conv_w"][l], p["conv_b"][l], p["gate_w"][l], p["gate_b"][l], p["rg_lambda"][l])
            xy_p3 = xy_p.reshape(B, T, 2 * C)
            hg_p, h_p = _rglru_seq(xy_p3, jnp.zeros((B, C), F32), jnp.zeros((B, nw - 1, C), F32), *args)
            hg_s, h_s = _rglru_step(xy_s, h0_s[l], conv0_s[l], *args)
            hs_p.append(h_p)
            hs_s.append(h_s)
            convs_p.append(xy_p3[:, T - (nw - 1):, :C])
            convs_s.append(jnp.concatenate([conv0_s[l], xy_s[:, None, :C]], axis=1)[:, 1:])
            m_p, m_s = _mm_fullk(hg_p.reshape(M, C), p["w_out_a"], l, a2=hg_s, epilogue="f32")
        else:
            j = l - n_a
            q_p, q_s = _mm_fullk(xnp, p["w_q"], j, a2=xns, epilogue="bf16", epilogue2="f32")
            o_p = _fox_prompt(q_p.reshape(B, T, d_attn), kv["kb"].reshape(B, T, d_attn), kv["vt"],
                              kv["cum_col"], kv["cum_row"], H)
            o_s = _fox_decode(q_s.reshape(DB, H, hd), kv["k_s"].reshape(DB, H, hd), kv["v_s"].reshape(DB, H, hd),
                              cache_k, cache_v, page_table, kv["cum_past"], kv["cum_new"])
            m_p, m_s = _mm_fullk(o_p.reshape(M, d_attn), p["w_o"], j, a2=o_s.reshape(DB, d_attn), epilogue="f32")
        xp, (xn2p,) = _resid_norm(xp, m_p, p["norm_mix_post"][l], [p["norm_mlp_pre"][l]])
        xs, (xn2s,) = _resid_norm(xs, m_s, p["norm_mix_post"][l], [p["norm_mlp_pre"][l]])
        hmid_p, hmid_s = _mm_fullk(xn2p, p["w_up"], l, a2=xn2s, epilogue="relu2_bf16")
        f_p, f_s = _mm_ktiled(hmid_p, p["w_down"], l, a2=hmid_s)
        if l + 1 == n_a:
            gains = [p["norm_kv"], p["norm_mix_pre"][l + 1]]
            xp, (xkv_p, xnp) = _resid_norm(xp, f_p, p["norm_mlp_post"][l], gains)
            xs, (xkv_s, xns) = _resid_norm(xs, f_s, p["norm_mlp_post"][l], gains)
            kv = _shared_kv(xkv_p, xkv_s, B, T, H, hd, cache_logf, page_table, p)
        elif l + 1 < depth:
            xp, (xnp,) = _resid_norm(xp, f_p, p["norm_mlp_post"][l], [p["norm_mix_pre"][l + 1]])
            xs, (xns,) = _resid_norm(xs, f_s, p["norm_mlp_post"][l], [p["norm_mix_pre"][l + 1]])
        else:
            xp, _ = _resid_norm(xp, f_p, p["norm_mlp_post"][l], [])
            xs, _ = _resid_norm(xs, f_s, p["norm_mlp_post"][l], [])
    return (xp.reshape(B, T, D), xs.reshape(DB, DT, D),
            jnp.stack(hs_p), jnp.stack(convs_p),
            kv["k_p"].reshape(B, T, H, hd), kv["v_p"].reshape(B, T, H, hd), kv["logf_p"],
            jnp.stack(hs_s), jnp.stack(convs_s),
            kv["k_s"].reshape(DB, DT, H, hd), kv["v_s"].reshape(DB, DT, H, hd), kv["logf_s"])


def _shared_kv(xkv_p, xkv_s, B, T, H, hd, cache_logf, page_table, p):
    d_attn = H * hd
    DB = xkv_s.shape[0]
    L = V7X_LANES
    w_kvf = jnp.swapaxes(p["w_kvf"], 0, 1)
    tn = min(MM_FULLK_TILE[1], d_attn)
    (k_p, kb), k_s = _mm_fullk(xkv_p, w_kvf, a2=xkv_s, n_cols=d_attn, col_block0=0,
                               epilogue="f32_bf16", epilogue2="f32", w_t=True)
    (v_p, vt), v_s = _mm_fullk(xkv_p, w_kvf, a2=xkv_s, n_cols=d_attn, col_block0=d_attn // tn,
                               epilogue="f32_bf16t", epilogue2="f32", w_t=True)
    assert (2 * d_attn) % L == 0 and H <= L
    b_f = jnp.pad(p["b_f"], (0, L - H)).reshape(1, L)
    f_p, f_s = _mm_fullk(xkv_p, w_kvf, a2=xkv_s, n_cols=L, col_block0=2 * d_attn // L, epilogue="f32",
                         w_t=True)
    logf_p, cum_col, cum_row = _logf_cum(f_p.reshape(B, T, L), b_f, H)
    cum_past, tot = _paged_cum(cache_logf, page_table)
    logf_s, cum_new = _logf_new(f_s, b_f, tot)
    return dict(k_p=k_p, v_p=v_p, kb=kb, vt=vt, logf_p=logf_p[:, :, :H], cum_col=cum_col, cum_row=cum_row,
                k_s=k_s, v_s=v_s, logf_s=logf_s[:, :H].reshape(DB, 1, H), cum_past=cum_past, cum_new=cum_new)


def _logf_new_kernel(f_ref, bf_ref, tot_ref, lf_ref, cn_ref):
    H = tot_ref.shape[1]
    lp = _log_sigmoid(_gate_logits(f_ref[...], H) + bf_ref[...])
    lf_ref[...] = lp
    cn_ref[...] = tot_ref[...] + lp[:, :H]


def _logf_new(f, b_f_pad, tot):
    DB, L = f.shape
    H = tot.shape[1]
    return pl.pallas_call(
        _logf_new_kernel,
        out_shape=[jax.ShapeDtypeStruct((DB, L), F32), jax.ShapeDtypeStruct((DB, H), F32)],
        compiler_params=pltpu.CompilerParams(vmem_limit_bytes=VMEM_LIMIT),
        name="logf_new",
    )(f, b_f_pad, tot)


def kernel(x_prompt, x_sample, state_rglru_h, state_conv, cache_k, cache_v, cache_logf, page_table,
           w_in, conv_w, conv_b, gate_w, gate_b, rg_lambda, w_out_a, norm_kv, w_kvf, b_f, w_q, w_o,
           norm_mix_pre, norm_mix_post, norm_mlp_pre, norm_mlp_post, w_up, w_down):
    p = dict(w_in=w_in, conv_w=conv_w, conv_b=conv_b, gate_w=gate_w, gate_b=gate_b, rg_lambda=rg_lambda,
             w_out_a=w_out_a, norm_kv=norm_kv, w_kvf=w_kvf, b_f=b_f, w_q=w_q, w_o=w_o,
             norm_mix_pre=norm_mix_pre, norm_mix_post=norm_mix_post, norm_mlp_pre=norm_mlp_pre,
             norm_mlp_post=norm_mlp_post, w_up=w_up, w_down=w_down)
    return _forward(x_prompt, x_sample, state_rglru_h, state_conv, cache_k, cache_v, cache_logf, page_table, p)
```

```python
import functools
import math

import jax
import jax.numpy as jnp
from jax import lax
from jax.experimental import pallas as pl
from jax.experimental.pallas import tpu as pltpu

F32 = jnp.float32
BF16 = jnp.bfloat16

RMS_EPS = 1e-6
LRU_C = 8.0
LOG2E = math.log2(math.e)

V7X_VMEM_BYTES = 64 * 1024 * 1024
V7X_LANES = 128
V7X_SUBLANES = 8
VMEM_LIMIT = V7X_VMEM_BYTES - 2 * 1024 * 1024
MM_SPILL_ALLOWANCE = 5 * 1024 * 1024
NORM_ROWS = 256


def _params(*sem):
    return pltpu.CompilerParams(dimension_semantics=sem, vmem_limit_bytes=VMEM_LIMIT)


def _rms_scale(x):
    return x * lax.rsqrt(jnp.mean(x * x, axis=-1, keepdims=True) + RMS_EPS)


def _log_sigmoid(x):
    return jnp.minimum(x, 0.0) - jnp.log1p(jnp.exp(-jnp.abs(x)))


def _gelu_tanh(x):
    c = math.sqrt(2.0 / math.pi)
    return 0.5 * x * (1.0 + jnp.tanh(c * (x + 0.044715 * (x * x * x))))


def _norm_cast_kernel(x_ref, g_ref, o_ref):
    o_ref[...] = (_rms_scale(x_ref[...]) * g_ref[...]).astype(o_ref.dtype)


def _norm_cast(x, g):
    M, D = x.shape
    tr = min(NORM_ROWS, M)
    return pl.pallas_call(
        _norm_cast_kernel,
        grid=(M // tr,),
        in_specs=[pl.BlockSpec((tr, D), lambda i: (i, 0)), pl.BlockSpec((1, D), lambda i: (0, 0))],
        out_specs=pl.BlockSpec((tr, D), lambda i: (i, 0)),
        out_shape=jax.ShapeDtypeStruct((M, D), BF16),
        compiler_params=_params("parallel"),
        name="norm_cast",
    )(x, g.reshape(1, D))


def _resid_norm_kernel(x_ref, m_ref, gp_ref, gn_ref, xo_ref, *n_refs):
    xn = x_ref[...] + _rms_scale(m_ref[...]) * gp_ref[...]
    xo_ref[...] = xn
    if n_refs:
        y = _rms_scale(xn)
        for j, r in enumerate(n_refs):
            r[...] = (y * gn_ref[j:j + 1, :]).astype(r.dtype)


def _resid_norm(x, m, g_post, g_next):
    M, D = x.shape
    n = len(g_next)
    tr = min(NORM_ROWS, M)
    gn = jnp.stack(g_next) if n else jnp.zeros((1, D), F32)
    row = pl.BlockSpec((tr, D), lambda i: (i, 0))
    outs = pl.pallas_call(
        _resid_norm_kernel,
        grid=(M // tr,),
        in_specs=[row, row, pl.BlockSpec((1, D), lambda i: (0, 0)),
                  pl.BlockSpec((gn.shape[0], D), lambda i: (0, 0))],
        out_specs=[row] * (1 + n),
        out_shape=[jax.ShapeDtypeStruct((M, D), F32)] + [jax.ShapeDtypeStruct((M, D), BF16)] * n,
        compiler_params=_params("parallel"),
        name="resid_norm",
    )(x, m, g_post.reshape(1, D), gn)
    return outs[0], list(outs[1:])


def _mm_store(acc, o_refs, epilogue):
    if epilogue == "f32":
        o_refs[0][...] = acc
    elif epilogue == "bf16":
        o_refs[0][...] = acc.astype(BF16)
    elif epilogue == "f32_bf16":
        o_refs[0][...] = acc
        o_refs[1][...] = acc.astype(BF16)
    elif epilogue == "f32_bf16t":
        o_refs[0][...] = acc
        o_refs[1][...] = acc.T.astype(BF16)
    elif epilogue == "relu2_bf16":
        h = jnp.maximum(acc, 0.0)
        o_refs[0][...] = (h * h).astype(BF16)
    else:
        raise ValueError(epilogue)


MM_FULLK_TILE = (2048, 512)
MM_KTILED_TILE = (2048, 1024, 2048)
MM_KTILED_OUT_BUFFERS = 1


def _fullk_rows(M, K, tn, out_dtypes):
    tm = min(MM_FULLK_TILE[0], M)
    while True:
        need = (2 * tm * K * 2 + 2 * K * tn * 4 + K * tn * 2
                + sum(2 * tm * tn * jnp.dtype(d).itemsize for d in out_dtypes) + MM_SPILL_ALLOWANCE)
        if need <= VMEM_LIMIT or tm % 2 or tm <= V7X_SUBLANES:
            return tm
        tm //= 2

_EPILOGUE_DTYPES = {"f32": (F32,), "bf16": (BF16,), "f32_bf16": (F32, BF16), "f32_bf16t": (F32, BF16),
                    "relu2_bf16": (BF16,)}


def _mm_fullk_kernel(*refs, epilogue, epilogue2, w_t, n_valid):
    n1 = len(_EPILOGUE_DTYPES[epilogue])
    if epilogue2 is None:
        a_ref, w_ref, *o_refs, wb_ref = refs
    else:
        a_ref, a2_ref, w_ref, *o_refs, wb_ref = refs
    dims = (((1,), (1 if w_t else 0,)), ((), ()))

    def product(lhs_ref):
        acc = lax.dot_general(lhs_ref[...], wb_ref[...], dims, preferred_element_type=F32)
        if n_valid is not None:
            col = pl.program_id(0) * acc.shape[1] + lax.broadcasted_iota(jnp.int32, acc.shape, 1)
            acc = jnp.where(col < n_valid, acc, 0.0)
        return acc

    @pl.when(pl.program_id(1) == 0)
    def _():
        wb_ref[...] = w_ref[...].astype(BF16)
        if epilogue2 is not None:
            _mm_store(product(a2_ref), o_refs[n1:], epilogue2)

    _mm_store(product(a_ref), o_refs[:n1], epilogue)


def _mm_fullk(a, w, layer=0, *, a2=None, n_cols=None, col_block0=0, epilogue="f32", epilogue2=None,
              w_t=False, n_valid=None):
    if w.ndim == 2:
        w = w[None]
    M, K = a.shape
    N = w.shape[1 if w_t else 2] if n_cols is None else n_cols
    dts = _EPILOGUE_DTYPES[epilogue]
    tn = min(MM_FULLK_TILE[1], N)
    tm = _fullk_rows(M, K, tn, dts)
    assert M % tm == 0 and N % tn == 0
    in_specs = [pl.BlockSpec((tm, K), lambda n, m: (m, 0))]
    out_specs = [pl.BlockSpec((tm, tn), lambda n, m: (m, n))] * len(dts)
    out_shape = [jax.ShapeDtypeStruct((M, N), d) for d in dts]
    if epilogue == "f32_bf16t":
        out_specs[1] = pl.BlockSpec((tn, tm), lambda n, m: (n, m))
        out_shape[1] = jax.ShapeDtypeStruct((N, M), dts[1])
    args = [a]
    if a2 is not None:
        epilogue2 = epilogue2 or epilogue
        M2 = a2.shape[0]
        dts2 = _EPILOGUE_DTYPES[epilogue2]
        in_specs.append(pl.BlockSpec((M2, K), lambda n, m: (0, 0)))
        out_specs += [pl.BlockSpec((M2, tn), lambda n, m: (0, n))] * len(dts2)
        out_shape += [jax.ShapeDtypeStruct((M2, N), d) for d in dts2]
        args.append(a2)
    if w_t:
        in_specs.append(pl.BlockSpec((None, tn, K), lambda n, m: (layer, n + col_block0, 0)))
    else:
        in_specs.append(pl.BlockSpec((None, K, tn), lambda n, m: (layer, 0, n + col_block0)))
    outs = pl.pallas_call(
        functools.partial(_mm_fullk_kernel, epilogue=epilogue, epilogue2=epilogue2 if a2 is not None else None,
                          w_t=w_t, n_valid=n_valid),
        grid=(N // tn, M // tm),
        in_specs=in_specs,
        out_specs=out_specs,
        out_shape=out_shape,
        scratch_shapes=[pltpu.VMEM((tn, K) if w_t else (K, tn), BF16)],
        compiler_params=_params("arbitrary", "arbitrary"),
        name="mm_fullk_" + epilogue,
    )(*args, w)
    unpack = lambda o: o if len(o) > 1 else o[0]
    if a2 is None:
        return unpack(outs)
    return unpack(outs[:len(dts)]), unpack(outs[len(dts):])


def _mm_ktiled_kernel(*refs, with2):
    if with2:
        a_ref, a2_ref, w_ref, o_ref, o2_ref = refs
    else:
        a_ref, w_ref, o_ref = refs
    first_k = pl.program_id(2) == 0
    wb = w_ref[...].astype(BF16)
    part = jnp.dot(a_ref[...], wb, preferred_element_type=F32)

    @pl.when(first_k)
    def _():
        o_ref[...] = part

    @pl.when(jnp.logical_not(first_k))
    def _():
        o_ref[...] += part

    if with2:
        first_m = pl.program_id(0) == 0

        @pl.when(first_m & first_k)
        def _():
            o2_ref[...] = jnp.dot(a2_ref[...], wb, preferred_element_type=F32)

        @pl.when(first_m & jnp.logical_not(first_k))
        def _():
            o2_ref[...] += jnp.dot(a2_ref[...], wb, preferred_element_type=F32)


def _mm_ktiled(a, w, layer, a2=None):
    M, K = a.shape
    N = w.shape[2]
    tm = min(MM_KTILED_TILE[0], M)
    tn = min(MM_KTILED_TILE[1], N)
    tk = min(MM_KTILED_TILE[2], K)
    assert M % tm == 0 and N % tn == 0 and K % tk == 0
    in_specs = [pl.BlockSpec((tm, tk), lambda m, n, k: (m, k))]
    out_specs = [pl.BlockSpec((tm, tn), lambda m, n, k: (m, n), pipeline_mode=pl.Buffered(MM_KTILED_OUT_BUFFERS))]
    out_shape = [jax.ShapeDtypeStruct((M, N), F32)]
    args = [a]
    if a2 is not None:
        M2 = a2.shape[0]
        last_n, last_k = N // tn - 1, K // tk - 1
        in_specs.append(pl.BlockSpec((M2, tk), lambda m, n, k: (0, jnp.where(m == 0, k, last_k))))
        out_specs.append(pl.BlockSpec((M2, tn), lambda m, n, k: (0, jnp.where(m == 0, n, last_n))))
        out_shape.append(jax.ShapeDtypeStruct((M2, N), F32))
        args.append(a2)
    in_specs.append(pl.BlockSpec((None, tk, tn), lambda m, n, k: (layer, k, n)))
    outs = pl.pallas_call(
        functools.partial(_mm_ktiled_kernel, with2=a2 is not None),
        grid=(M // tm, N // tn, K // tk),
        in_specs=in_specs,
        out_specs=out_specs,
        out_shape=out_shape,
        compiler_params=_params("arbitrary", "arbitrary", "arbitrary"),
        name="mm_ktiled",
    )(*args, w)
    return outs[0] if a2 is None else (outs[0], outs[1])


def _rglru_coeffs(conv, gw_ref, gb_ref, lam_ref):
    cb = conv.astype(BF16)
    gi = jnp.dot(cb, gw_ref[0, 0].astype(BF16), preferred_element_type=F32) + gb_ref[0:1, :]
    gr = jnp.dot(cb, gw_ref[1, 0].astype(BF16), preferred_element_type=F32) + gb_ref[1:2, :]
    i_gate = jax.nn.sigmoid(gi)
    r_gate = jax.nn.sigmoid(gr)
    log_a = LRU_C * r_gate * _log_sigmoid(lam_ref[...])
    a = jnp.exp(log_a)
    th = jnp.tanh(log_a)
    mult = jnp.sqrt(-2.0 * th / (1.0 - th))
    return a, mult * i_gate * conv


RGLRU_ROWS = 1024


def _rglru_seq_kernel(x_ref, y_ref, h0_ref, c0_ref, cw_ref, cb_ref, gw_ref, gb_ref, lam_ref,
                      hg_ref, hl_ref, xs_ref, hc_ref, *, tT):
    t = pl.program_id(2)
    S = V7X_SUBLANES

    @pl.when(t == 0)
    def _():
        xs_ref[0:S, :] = c0_ref[0]
        hc_ref[...] = h0_ref[0]

    x = x_ref[0]
    xs_ref[S:S + tT, :] = x
    cw = cw_ref[...]
    nw = cw.shape[0]
    conv = cb_ref[...] + cw[nw - 1:nw, :] * x
    for j in range(1, nw):
        conv = conv + cw[nw - 1 - j:nw - j, :] * xs_ref[S - j:S - j + tT, :]
    xs_ref[0:S, :] = xs_ref[tT:tT + S, :]

    a, b = _rglru_coeffs(conv, gw_ref, gb_ref, lam_ref)

    row = lax.broadcasted_iota(jnp.int32, (S, a.shape[1]), 0)
    h = hc_ref[...]
    hs = []
    for r in range(0, tT, S):
        a8 = a[r:r + S]
        b8 = b[r:r + S]
        s = 1
        while s < S:
            keep = row >= s
            b8 = jnp.where(keep, a8 * pltpu.roll(b8, s, 0) + b8, b8)
            a8 = jnp.where(keep, a8 * pltpu.roll(a8, s, 0), a8)
            s *= 2
        h8 = a8 * h + b8
        hs.append(h8)
        h = h8[S - 1:S, :]
    hc_ref[...] = h
    hl_ref[0] = h
    hg_ref[0] = (jnp.concatenate(hs, axis=0) * _gelu_tanh(y_ref[0])).astype(hg_ref.dtype)


def _rglru_seq(xy, h0, conv0, conv_w, conv_b, gate_w, gate_b, lam):
    B, T, C2 = xy.shape
    C = C2 // 2
    nblk, cblk = gate_w.shape[1], gate_w.shape[2]
    nw = conv_w.shape[0]
    S = V7X_SUBLANES
    tT = min(RGLRU_ROWS, T)
    assert T % tT == 0 and tT % S == 0 and nw - 1 <= S and cblk * nblk == C
    c0 = jnp.concatenate([jnp.zeros((B, S - (nw - 1), C), F32), conv0], axis=1)
    hg, hl = pl.pallas_call(
        functools.partial(_rglru_seq_kernel, tT=tT),
        grid=(B, nblk, T // tT),
        in_specs=[
            pl.BlockSpec((1, tT, cblk), lambda b, n, t: (b, t, n)),
            pl.BlockSpec((1, tT, cblk), lambda b, n, t: (b, t, n + nblk)),
            pl.BlockSpec((1, 1, cblk), lambda b, n, t: (b, 0, n)),
            pl.BlockSpec((1, S, cblk), lambda b, n, t: (b, 0, n)),
            pl.BlockSpec((nw, cblk), lambda b, n, t: (0, n)),
            pl.BlockSpec((1, cblk), lambda b, n, t: (0, n)),
            pl.BlockSpec((2, 1, cblk, cblk), lambda b, n, t: (0, n, 0, 0)),
            pl.BlockSpec((2, cblk), lambda b, n, t: (0, n)),
            pl.BlockSpec((1, cblk), lambda b, n, t: (0, n)),
        ],
        out_specs=[pl.BlockSpec((1, tT, cblk), lambda b, n, t: (b, t, n)),
                   pl.BlockSpec((1, 1, cblk), lambda b, n, t: (b, 0, n))],
        out_shape=[jax.ShapeDtypeStruct((B, T, C), BF16), jax.ShapeDtypeStruct((B, 1, C), F32)],
        scratch_shapes=[pltpu.VMEM((tT + S, cblk), F32), pltpu.VMEM((1, cblk), F32)],
        compiler_params=_params("parallel", "parallel", "arbitrary"),
        name="rglru_seq",
    )(xy, xy, h0.reshape(B, 1, C), c0, conv_w, conv_b.reshape(1, C), gate_w, gate_b, lam.reshape(1, C))
    return hg, hl.reshape(B, C)


def _rglru_step_kernel(x_ref, y_ref, h0_ref, c0_ref, cw_ref, cb_ref, gw_ref, gb_ref, lam_ref,
                       hg_ref, hl_ref):
    x = x_ref[...]
    cw = cw_ref[...]
    nw = cw.shape[0]
    conv = cb_ref[...] + cw[nw - 1:nw, :] * x
    for j in range(1, nw):
        conv = conv + cw[nw - 1 - j:nw - j, :] * c0_ref[nw - 1 - j]
    a, b = _rglru_coeffs(conv, gw_ref, gb_ref, lam_ref)
    h = a * h0_ref[...] + b
    hl_ref[...] = h
    hg_ref[...] = (h * _gelu_tanh(y_ref[...])).astype(hg_ref.dtype)


def _rglru_step(xy, h0, conv0, conv_w, conv_b, gate_w, gate_b, lam):
    B, C2 = xy.shape
    C = C2 // 2
    nblk, cblk = gate_w.shape[1], gate_w.shape[2]
    nw = conv_w.shape[0]
    c0 = jnp.swapaxes(conv0, 0, 1)
    return pl.pallas_call(
        _rglru_step_kernel,
        grid=(nblk,),
        in_specs=[
            pl.BlockSpec((B, cblk), lambda n: (0, n)),
            pl.BlockSpec((B, cblk), lambda n: (0, n + nblk)),
            pl.BlockSpec((B, cblk), lambda n: (0, n)),
            pl.BlockSpec((nw - 1, B, cblk), lambda n: (0, 0, n)),
            pl.BlockSpec((nw, cblk), lambda n: (0, n)),
            pl.BlockSpec((1, cblk), lambda n: (0, n)),
            pl.BlockSpec((2, 1, cblk, cblk), lambda n: (0, n, 0, 0)),
            pl.BlockSpec((2, cblk), lambda n: (0, n)),
            pl.BlockSpec((1, cblk), lambda n: (0, n)),
        ],
        out_specs=[pl.BlockSpec((B, cblk), lambda n: (0, n)), pl.BlockSpec((B, cblk), lambda n: (0, n))],
        out_shape=[jax.ShapeDtypeStruct((B, C), BF16), jax.ShapeDtypeStruct((B, C), F32)],
        compiler_params=_params("parallel"),
        name="rglru_step",
    )(xy, xy, h0, c0, conv_w, conv_b.reshape(1, C), gate_w, gate_b, lam.reshape(1, C))


def _tri_dots(lp, carry_col_ref, carry_row_ref, cum_col_ref, cum_row_ref):
    P = lp.shape[0]
    hi = lax.Precision.HIGHEST
    r = lax.broadcasted_iota(jnp.int32, (P, P), 0)
    c = lax.broadcasted_iota(jnp.int32, (P, P), 1)
    ones = jnp.ones((P, P), F32)
    tril = (c <= r).astype(F32)
    cum_col_ref[0] = jnp.dot(tril, lp, precision=hi, preferred_element_type=F32) + carry_col_ref[...]
    carry_col_ref[...] += jnp.dot(ones, lp, precision=hi, preferred_element_type=F32)
    lpt = lp.T
    triu = (r <= c).astype(F32)
    cum_row_ref[0] = jnp.dot(lpt, triu, precision=hi, preferred_element_type=F32) + carry_row_ref[...]
    carry_row_ref[...] += jnp.dot(lpt, ones, precision=hi, preferred_element_type=F32)


def _logf_cum_kernel(f_ref, bf_ref, lf_ref, cc_ref, cr_ref, car_c, car_r):
    @pl.when(pl.program_id(1) == 0)
    def _():
        car_c[...] = jnp.zeros_like(car_c)
        car_r[...] = jnp.zeros_like(car_r)

    lp = _log_sigmoid(f_ref[0] + bf_ref[...])
    lf_ref[0] = lp
    _tri_dots(lp, car_c, car_r, cc_ref, cr_ref)


def _logf_cum(f, b_f_pad):
    B, T, L = f.shape
    P = V7X_LANES
    assert T % P == 0 and L == P
    return pl.pallas_call(
        _logf_cum_kernel,
        grid=(B, T // P),
        in_specs=[pl.BlockSpec((1, P, L), lambda b, t: (b, t, 0)), pl.BlockSpec((1, L), lambda b, t: (0, 0))],
        out_specs=[pl.BlockSpec((1, P, L), lambda b, t: (b, t, 0)),
                   pl.BlockSpec((1, P, L), lambda b, t: (b, t, 0)),
                   pl.BlockSpec((1, L, P), lambda b, t: (b, 0, t))],
        out_shape=[jax.ShapeDtypeStruct((B, T, L), F32), jax.ShapeDtypeStruct((B, T, L), F32),
                   jax.ShapeDtypeStruct((B, L, T), F32)],
        scratch_shapes=[pltpu.VMEM((P, L), F32), pltpu.VMEM((L, P), F32)],
        compiler_params=_params("parallel", "arbitrary"),
        name="logf_cum",
    )(f, b_f_pad)


def _paged_cum_kernel(pt_ref, *refs, n_in):
    lf_refs = refs[:n_in]
    cc_ref, tot_ref, car = refs[n_in:]

    @pl.when(pl.program_id(1) == 0)
    def _():
        car[...] = jnp.zeros_like(car)

    P = lf_refs[0].shape[1]
    hi = lax.Precision.HIGHEST
    tril = (lax.broadcasted_iota(jnp.int32, (P, P), 1) <= lax.broadcasted_iota(jnp.int32, (P, P), 0)).astype(F32)
    for g, lf_ref in enumerate(lf_refs):
        lp = lf_ref[0]
        cc_ref[0, g * P:(g + 1) * P, :] = jnp.dot(tril, lp, precision=hi, preferred_element_type=F32) + car[...]
        car[...] += jnp.sum(lp, axis=0, keepdims=True)
    tot_ref[0] = car[0:V7X_SUBLANES, :]


def _paged_cum(cache_logf, page_table):
    n_pool, P, H = cache_logf.shape
    DB, n_pages = page_table.shape
    G = math.gcd(n_pages, 8)
    S = V7X_SUBLANES
    specs = [pl.BlockSpec((1, P, H), lambda b, s, pt, g=g: (pt[b, s * G + g], 0, 0)) for g in range(G)]
    cum, tot = pl.pallas_call(
        functools.partial(_paged_cum_kernel, n_in=G),
        grid_spec=pltpu.PrefetchScalarGridSpec(
            num_scalar_prefetch=1,
            grid=(DB, n_pages // G),
            in_specs=specs,
            out_specs=[pl.BlockSpec((1, G * P, H), lambda b, s, pt: (b, s, 0)),
                       pl.BlockSpec((1, S, H), lambda b, s, pt: (b, 0, 0))],
            scratch_shapes=[pltpu.VMEM((P, H), F32)],
        ),
        out_shape=[jax.ShapeDtypeStruct((DB, n_pages * P, H), F32), jax.ShapeDtypeStruct((DB, S, H), F32)],
        compiler_params=_params("parallel", "arbitrary"),
        name="paged_cum",
    )(page_table, *([cache_logf] * G))
    return cum, tot[:, 0, :]


FOX_HEADS_PER_STEP = 8


def _fox_prompt_kernel(q_ref, k_ref, vt_ref, cq_ref, ck_ref, o_ref, *, tq, hd, hb, nq, scale):
    qi = pl.program_id(2)
    nt = (((1,), (1,)), ((), ()))
    below_or_on = (lax.broadcasted_iota(jnp.int32, (tq, 1), 0)
                   <= lax.broadcasted_iota(jnp.int32, (1, tq), 1))

    def q_block(c):
        lo = c * tq
        heads = [slice(j * hd, (j + 1) * hd) for j in range(hb)]
        scale2 = scale * LOG2E
        scores = []
        for j, cols in enumerate(heads):
            qj = q_ref[0, :, cols]
            cqj = cq_ref[0, 0, j:j + 1, :] * LOG2E
            sd = lax.dot_general(k_ref[0, lo:lo + tq, cols], qj, nt, preferred_element_type=F32) * scale2
            sd = sd + (cqj - ck_ref[0, 0, lo:lo + tq, j:j + 1] * LOG2E)
            sd = jnp.where(below_or_on, sd, -jnp.inf)
            m = jnp.max(sd, axis=0, keepdims=True)
            sp = None
            if c:
                sp = lax.dot_general(k_ref[0, 0:lo, cols], qj, nt, preferred_element_type=F32) * scale2
                sp = sp + (cqj - ck_ref[0, 0, 0:lo, j:j + 1] * LOG2E)
                m = jnp.maximum(m, jnp.max(sp, axis=0, keepdims=True))
            scores.append((sd, sp, m))
        probs = []
        for sd, sp, m in scores:
            pd = jnp.exp2(sd - m)
            l = jnp.sum(pd, axis=0, keepdims=True)
            pp = None
            if c:
                pp = jnp.exp2(sp - m)
                l = l + jnp.sum(pp, axis=0, keepdims=True)
                pp = pp.astype(BF16)
            probs.append((pd.astype(BF16), pp, l))
        for cols, (pd, pp, l) in zip(heads, probs):
            acc = jnp.dot(vt_ref[cols, lo:lo + tq], pd, preferred_element_type=F32)
            if c:
                acc = acc + jnp.dot(vt_ref[cols, 0:lo], pp, preferred_element_type=F32)
            o_ref[0, :, cols] = (acc / l).T.astype(o_ref.dtype)

    for c in range(nq):
        pl.when(qi == c)(functools.partial(q_block, c))


def _fox_prompt(q, k, vt, cum_col, cum_row, n_heads):
    B, T, D = q.shape
    hd = D // n_heads
    hb = min(FOX_HEADS_PER_STEP, n_heads)
    hg = n_heads // hb
    tq = min(256, T)
    assert n_heads % hb == 0 and T % tq == 0
    cq = cum_row[:, :n_heads, :].reshape(B, hg, hb, T)
    ck = jnp.transpose(cum_col[:, :, :n_heads].reshape(B, T, hg, hb), (0, 2, 1, 3))
    return pl.pallas_call(
        functools.partial(_fox_prompt_kernel, tq=tq, hd=hd, hb=hb, nq=T // tq, scale=hd ** -0.5),
        grid=(B, hg, T // tq),
        in_specs=[pl.BlockSpec((1, tq, hb * hd), lambda b, g, i: (b, i, g)),
                  pl.BlockSpec((1, T, hb * hd), lambda b, g, i: (b, 0, g)),
                  pl.BlockSpec((hb * hd, T), lambda b, g, i: (g, b)),
                  pl.BlockSpec((1, 1, hb, tq), lambda b, g, i: (b, g, 0, i)),
                  pl.BlockSpec((1, 1, T, hb), lambda b, g, i: (b, g, 0, 0))],
        out_specs=pl.BlockSpec((1, tq, hb * hd), lambda b, g, i: (b, i, g)),
        out_shape=jax.ShapeDtypeStruct((B, T, D), BF16),
        compiler_params=_params("parallel", "parallel", "arbitrary"),
        name="fox_prompt",
    )(q, k, vt, cq, ck)


DECODE_PAGES_PER_STEP = 4


def _fox_decode_kernel(pt_ref, q_ref, *refs, n_pg, scale):
    k_refs, v_refs = refs[:n_pg], refs[n_pg:2 * n_pg]
    ckf_ref, cqf_ref, cqb_ref, kn_ref, vn_ref, o_ref, m_ref, l_ref, acc_ref = refs[2 * n_pg:]
    p_idx = pl.program_id(1)
    H = q_ref.shape[1]

    @pl.when(p_idx == 0)
    def _():
        m_ref[...] = jnp.full_like(m_ref, -jnp.inf)
        l_ref[...] = jnp.zeros_like(l_ref)
        acc_ref[...] = jnp.zeros_like(acc_ref)

    qb = q_ref[0].astype(BF16)
    rows = k_refs[0].shape[1]
    own = (lax.broadcasted_iota(jnp.int32, (H, rows), 1) % H) == lax.broadcasted_iota(jnp.int32, (H, rows), 0)
    m, l, acc = m_ref[...], l_ref[...], acc_ref[...]
    for g in range(n_pg):
        kb = k_refs[g][0].astype(BF16)
        s = lax.dot_general(qb, kb, (((1,), (1,)), ((), ())), preferred_element_type=F32) * scale
        s = s + (cqf_ref[0] - ckf_ref[0, 0, g:g + 1, :])
        s = jnp.where(own, s, -jnp.inf)
        m_new = jnp.maximum(m, jnp.max(s, axis=1, keepdims=True))
        alpha = jnp.exp(m - m_new)
        p = jnp.exp(s - m_new[:, 0:1])
        l = alpha * l + jnp.sum(p, axis=1, keepdims=True)
        acc = alpha * acc + jnp.dot(p.astype(BF16), v_refs[g][0].astype(BF16), preferred_element_type=F32)
        m = m_new
    m_ref[...] = m
    l_ref[...] = l
    acc_ref[...] = acc

    @pl.when(p_idx == pl.num_programs(1) - 1)
    def _():
        qn = q_ref[0].astype(BF16).astype(F32)
        kn = kn_ref[0].astype(BF16).astype(F32)
        vn = vn_ref[0].astype(BF16).astype(F32)
        cq = cqb_ref[0]
        s_n = jnp.sum(qn * kn, axis=1, keepdims=True) * scale + (cq - cq)
        m_p = m_ref[...]
        m_f = jnp.maximum(m_p, s_n)
        al = jnp.exp(m_p - m_f)
        p_n = jnp.exp(s_n - m_f)
        l_f = al * l_ref[...] + p_n
        o = al * acc_ref[...] + p_n.astype(BF16).astype(F32) * vn
        o_ref[0] = (o / l_f).astype(o_ref.dtype)


def _fox_decode(q, k_new, v_new, cache_k, cache_v, page_table, cum_past, cum_new):
    DB, H, hd = q.shape
    n_pool, P = cache_k.shape[0], cache_k.shape[1]
    n_pages = page_table.shape[1]
    R = P * H
    assert hd == V7X_LANES
    k2 = cache_k.reshape(n_pool, R, hd)
    v2 = cache_v.reshape(n_pool, R, hd)
    G = math.gcd(n_pages, DECODE_PAGES_PER_STEP)
    ckf = cum_past.reshape(DB, n_pages // G, G, R)
    cqf = jnp.tile(cum_new, (1, P)).reshape(DB, 1, R)
    cqb = jnp.broadcast_to(cum_new[:, :, None], (DB, H, hd))
    per_seq = lambda shape: pl.BlockSpec((1,) + shape, lambda b, s, pt: (b, 0, 0))
    page = [pl.BlockSpec((1, R, hd), lambda b, s, pt, g=g: (pt[b, s * G + g], 0, 0)) for g in range(G)]
    return pl.pallas_call(
        functools.partial(_fox_decode_kernel, n_pg=G, scale=hd ** -0.5),
        grid_spec=pltpu.PrefetchScalarGridSpec(
            num_scalar_prefetch=1,
            grid=(DB, n_pages // G),
            in_specs=[per_seq((H, hd))] + page + page
                     + [pl.BlockSpec((1, 1, G, R), lambda b, s, pt: (b, s, 0, 0)),
                        per_seq((1, R)), per_seq((H, hd)), per_seq((H, hd)), per_seq((H, hd))],
            out_specs=per_seq((H, hd)),
            scratch_shapes=[pltpu.VMEM((H, hd), F32), pltpu.VMEM((H, hd), F32), pltpu.VMEM((H, hd), F32)],
        ),
        out_shape=jax.ShapeDtypeStruct((DB, H, hd), BF16),
        compiler_params=_params("parallel", "arbitrary"),
        name="fox_decode",
    )(page_table, q, *([k2] * G), *([v2] * G), ckf, cqf, cqb, k_new, v_new)


def _forward(x_prompt, x_sample, h0_s, conv0_s, cache_k, cache_v, cache_logf, page_table, p):
    B, T, D = x_prompt.shape
    DB, DT, _ = x_sample.shape
    assert DT == 1, "the sample group carries one new token per sequence"
    M = B * T
    depth = p["w_up"].shape[0]
    n_a = p["w_in"].shape[0]
    H = p["b_f"].shape[0]
    hd = p["w_q"].shape[2] // H
    d_attn = H * hd
    nw = p["conv_w"].shape[1]
    assert T >= nw - 1

    xp = x_prompt.reshape(M, D)
    xs = x_sample.reshape(DB, D)
    xnp = _norm_cast(xp, p["norm_mix_pre"][0])
    xns = _norm_cast(xs, p["norm_mix_pre"][0])
    hs_p, convs_p, hs_s, convs_s = [], [], [], []
    kv = None
    for l in range(depth):
        if l < n_a:
            xy_p, xy_s = _mm_fullk(xnp, p["w_in"], l, a2=xns, epilogue="f32")
            C = xy_p.shape[1] // 2
            args = (p["conv_w"][l], p["conv_b"][l], p["gate_w"][l], p["gate_b"][l], p["rg_lambda"][l])
            xy_p3 = xy_p.reshape(B, T, 2 * C)
            hg_p, h_p = _rglru_seq(xy_p3, jnp.zeros((B, C), F32), jnp.zeros((B, nw - 1, C), F32), *args)
            hg_s, h_s = _rglru_step(xy_s, h0_s[l], conv0_s[l], *args)
            hs_p.append(h_p)
            hs_s.append(h_s)
            convs_p.append(xy_p3[:, T - (nw - 1):, :C])
            convs_s.append(jnp.concatenate([conv0_s[l], xy_s[:, None, :C]], axis=1)[:, 1:])
            m_p, m_s = _mm_fullk(hg_p.reshape(M, C), p["w_out_a"], l, a2=hg_s, epilogue="f32")
        else:
            j = l - n_a
            q_p, q_s = _mm_fullk(xnp, p["w_q"], j, a2=xns, epilogue="bf16", epilogue2="f32")
            o_p = _fox_prompt(q_p.reshape(B, T, d_attn), kv["kb"].reshape(B, T, d_attn), kv["vt"],
                              kv["cum_col"], kv["cum_row"], H)
            o_s = _fox_decode(q_s.reshape(DB, H, hd), kv["k_s"].reshape(DB, H, hd), kv["v_s"].reshape(DB, H, hd),
                              cache_k, cache_v, page_table, kv["cum_past"], kv["cum_new"])
            m_p, m_s = _mm_fullk(o_p.reshape(M, d_attn), p["w_o"], j, a2=o_s.reshape(DB, d_attn), epilogue="f32")
        xp, (xn2p,) = _resid_norm(xp, m_p, p["norm_mix_post"][l], [p["norm_mlp_pre"][l]])
        xs, (xn2s,) = _resid_norm(xs, m_s, p["norm_mix_post"][l], [p["norm_mlp_pre"][l]])
        hmid_p, hmid_s = _mm_fullk(xn2p, p["w_up"], l, a2=xn2s, epilogue="relu2_bf16")
        f_p, f_s = _mm_ktiled(hmid_p, p["w_down"], l, a2=hmid_s)
        if l + 1 == n_a:
            gains = [p["norm_kv"], p["norm_mix_pre"][l + 1]]
            xp, (xkv_p, xnp) = _resid_norm(xp, f_p, p["norm_mlp_post"][l], gains)
            xs, (xkv_s, xns) = _resid_norm(xs, f_s, p["norm_mlp_post"][l], gains)
            kv = _shared_kv(xkv_p, xkv_s, B, T, H, hd, cache_logf, page_table, p)
        elif l + 1 < depth:
            xp, (xnp,) = _resid_norm(xp, f_p, p["norm_mlp_post"][l], [p["norm_mix_pre"][l + 1]])
            xs, (xns,) = _resid_norm(xs, f_s, p["norm_mlp_post"][l], [p["norm_mix_pre"][l + 1]])
        else:
            xp, _ = _resid_norm(xp, f_p, p["norm_mlp_post"][l], [])
            xs, _ = _resid_norm(xs, f_s, p["norm_mlp_post"][l], [])
    return (xp.reshape(B, T, D), xs.reshape(DB, DT, D),
            jnp.stack(hs_p), jnp.stack(convs_p),
            kv["k_p"].reshape(B, T, H, hd), kv["v_p"].reshape(B, T, H, hd), kv["logf_p"],
            jnp.stack(hs_s), jnp.stack(convs_s),
            kv["k_s"].reshape(DB, DT, H, hd), kv["v_s"].reshape(DB, DT, H, hd), kv["logf_s"])


def _shared_kv(xkv_p, xkv_s, B, T, H, hd, cache_logf, page_table, p):
    d_attn = H * hd
    DB = xkv_s.shape[0]
    L = V7X_LANES
    w_kvf = jnp.swapaxes(p["w_kvf"], 0, 1)
    tn = min(MM_FULLK_TILE[1], d_attn)
    (k_p, kb), k_s = _mm_fullk(xkv_p, w_kvf, a2=xkv_s, n_cols=d_attn, col_block0=0,
                               epilogue="f32_bf16", epilogue2="f32", w_t=True)
    (v_p, vt), v_s = _mm_fullk(xkv_p, w_kvf, a2=xkv_s, n_cols=d_attn, col_block0=d_attn // tn,
                               epilogue="f32_bf16t", epilogue2="f32", w_t=True)
    assert (2 * d_attn) % L == 0 and H <= L
    b_f = jnp.pad(p["b_f"], (0, L - H)).reshape(1, L)
    f_p, f_s = _mm_fullk(xkv_p, w_kvf, a2=xkv_s, n_cols=L, col_block0=2 * d_attn // L, epilogue="f32",
                         w_t=True, n_valid=H)
    logf_p, cum_col, cum_row = _logf_cum(f_p.reshape(B, T, L), b_f)
    cum_past, tot = _paged_cum(cache_logf, page_table)
    logf_s, cum_new = _logf_new(f_s, b_f, tot)
    return dict(k_p=k_p, v_p=v_p, kb=kb, vt=vt, logf_p=logf_p[:, :, :H], cum_col=cum_col, cum_row=cum_row,
                k_s=k_s, v_s=v_s, logf_s=logf_s[:, :H].reshape(DB, 1, H), cum_past=cum_past, cum_new=cum_new)


def _logf_new_kernel(f_ref, bf_ref, tot_ref, lf_ref, cn_ref):
    H = tot_ref.shape[1]
    lp = _log_sigmoid(f_ref[...] + bf_ref[...])
    lf_ref[...] = lp
    cn_ref[...] = tot_ref[...] + lp[:, :H]


def _logf_new(f, b_f_pad, tot):
    DB, L = f.shape
    H = tot.shape[1]
    return pl.pallas_call(
        _logf_new_kernel,
        out_shape=[jax.ShapeDtypeStruct((DB, L), F32), jax.ShapeDtypeStruct((DB, H), F32)],
        compiler_params=pltpu.CompilerParams(vmem_limit_bytes=VMEM_LIMIT),
        name="logf_new",
    )(f, b_f_pad, tot)


def kernel(x_prompt, x_sample, state_rglru_h, state_conv, cache_k, cache_v, cache_logf, page_table,
           w_in, conv_w, conv_b, gate_w, gate_b, rg_lambda, w_out_a, norm_kv, w_kvf, b_f, w_q, w_o,
           norm_mix_pre, norm_mix_post, norm_mlp_pre, norm_mlp_post, w_up, w_down):
    p = dict(w_in=w_in, conv_w=conv_w, conv_b=conv_b, gate_w=gate_w, gate_b=gate_b, rg_lambda=rg_lambda,
             w_out_a=w_out_a, norm_kv=norm_kv, w_kvf=w_kvf, b_f=b_f, w_q=w_q, w_o=w_o,
             norm_mix_pre=norm_mix_pre, norm_mix_post=norm_mix_post, norm_mlp_pre=norm_mlp_pre,
             norm_mlp_post=norm_mlp_post, w_up=w_up, w_down=w_down)
    return _forward(x_prompt, x_sample, state_rglru_h, state_conv, cache_k, cache_v, cache_logf, page_table, p)
```
